```python
import jax
import jax.numpy as jnp
from jax import lax
import numpy as np

D_MODEL = 1024
BATCH = 8
SEQ = 4096
DEPTH = 4

GRID_W = 64
CTX_LEN = 256
HEAD_DIM = 64
N_RWKV_HEADS = 8
D_RWKV = N_RWKV_HEADS * HEAD_DIM
D_POOL = D_MODEL - D_RWKV
POOL_WINDOWS = (2, 4, 8, 16)
POOL_GROUP = D_POOL // len(POOL_WINDOWS)
D_DECAY_LORA = 32
D_AAA_LORA = 64
D_GATE_LORA = 96
D_RWKV_PROJ = 3 * D_RWKV + 2 * D_DECAY_LORA + 2 * D_AAA_LORA + D_GATE_LORA
D_EVEN_IN = D_RWKV_PROJ + D_POOL
D_FF = 2816
RMS_EPS = 1e-6
GN_EPS = 64e-5

kernel_name = 'hybrid_rwkv7_pool_shortconv_dit_block'


def rmsnorm(x, g):
    xf = x.astype(jnp.float32)
    y = xf * lax.rsqrt(jnp.mean(xf * xf, axis=-1, keepdims=True) + RMS_EPS)
    return (y * g).astype(x.dtype)


def modulate(h, shift, scale):
    return h * (1.0 + scale) + shift


def neighbours(x, grid):
    n_rows, row_len = grid
    bsz, _, ch = x.shape
    xp = jnp.pad(x.reshape(bsz, n_rows, row_len, ch), ((0, 0), (0, 0), (1, 1), (0, 0)))
    return xp[:, :, :-2].reshape(x.shape), xp[:, :, 2:].reshape(x.shape)


def dwconv3(x, w, grid):
    prev, nxt = neighbours(x, grid)
    return prev * w[0] + x * w[1] + nxt * w[2]


def multiscale_pool(x, w_group, scale, grid):
    n_rows, row_len = grid
    bsz, _, ch = x.shape
    xr = x.reshape(bsz, n_rows, row_len, ch).astype(jnp.float32)
    cs = jnp.concatenate([jnp.zeros_like(xr[:, :, :1]), jnp.cumsum(xr, axis=2)], axis=2)
    pos = jnp.arange(row_len)
    outs = []
    for gi, win in enumerate(POOL_WINDOWS):
        sl = slice(gi * POOL_GROUP, (gi + 1) * POOL_GROUP)
        lo = jnp.clip(pos - win // 2, 0, row_len)
        hi = jnp.clip(pos + win // 2, 0, row_len)
        csg = cs[..., sl]
        mean = (jnp.take(csg, hi, axis=2) - jnp.take(csg, lo, axis=2)) / (hi - lo).astype(jnp.float32)[:, None]
        d = (mean - xr[..., sl]).astype(x.dtype)
        outs.append(jnp.einsum('brlc,cd->brld', d, w_group[gi]))
    return jnp.concatenate(outs, axis=-1).reshape(x.shape) * scale


def to_heads(t):
    return t.reshape(t.shape[:-1] + (N_RWKV_HEADS, HEAD_DIM))


def rwkv_inputs(pr, mu, w0, w_up, a0, a_up, g_up, k_k, k_a):
    bsz, seq, _ = pr.shape
    prev, nxt = neighbours(pr, (1, seq))
    pr = pr + mu[0] * (prev - pr) + mu[1] * (nxt - pr)
    o1, o2, o3 = D_RWKV, 2 * D_RWKV, 3 * D_RWKV
    o4 = o3 + 2 * D_DECAY_LORA
    o5 = o4 + 2 * D_AAA_LORA
    r, k, v = pr[..., :o1], pr[..., o1:o2], pr[..., o2:o3]
    lw = pr[..., o3:o4].reshape(bsz, seq, 2, D_DECAY_LORA)
    la = pr[..., o4:o5].reshape(bsz, seq, 2, D_AAA_LORA)
    lg = pr[..., o5:]
    w = w0 + jnp.einsum('bsdr,drc->bsdc', jnp.tanh(lw), w_up)
    w = -jax.nn.softplus(-w.astype(jnp.float32)) - 0.5
    decay = jnp.exp(-jnp.exp(w))
    a = jax.nn.sigmoid(a0 + jnp.einsum('bsdr,drc->bsdc', la, a_up))
    kk = to_heads((k * k_k).astype(jnp.float32))
    kk = kk * lax.rsqrt(jnp.maximum(jnp.sum(kk * kk, axis=-1, keepdims=True), 1e-24))
    kd = k[:, :, None, :] * (1.0 + (a - 1.0) * k_a)
    b = kk[:, :, None] * to_heads(a)
    g = jax.nn.sigmoid(lg) @ g_up
    return (to_heads(r), to_heads(v), kk, to_heads(decay), to_heads(kd), b, g)


def wkv_scan(state, r, v, kk, decay, kd, b, reverse):
    tm = lambda t: jnp.swapaxes(t.astype(jnp.float32), 0, 1)

    def step(S, inp):
        r_t, v_t, kk_t, w_t, k_t, b_t = inp
        sa = jnp.einsum('bhvk,bhk->bhv', S, kk_t)
        S = S * w_t[:, :, None, :] - sa[..., None] * b_t[:, :, None, :] + v_t[..., None] * k_t[:, :, None, :]
        return S, jnp.einsum('bhvk,bhk->bhv', S, r_t)

    state, ys = lax.scan(step, state, (tm(r), tm(v), tm(kk), tm(decay), tm(kd), tm(b)), reverse=reverse)
    return state, jnp.swapaxes(ys, 0, 1)


def rwkv_output(y, r, v, kd, r_k, gn_w, gn_b, g):
    bsz, seq = y.shape[:2]
    mean = jnp.mean(y, axis=-1, keepdims=True)
    yc = y - mean
    yn = yc * lax.rsqrt(jnp.mean(yc * yc, axis=-1, keepdims=True) + GN_EPS)
    yn = yn.reshape(bsz, seq, D_RWKV) * gn_w + gn_b
    coef = jnp.sum(r[:, :, None].astype(jnp.float32) * kd.astype(jnp.float32) * r_k, axis=(2, 4))
    bonus = (coef[..., None] * v).reshape(bsz, seq, D_RWKV)
    return ((yn + bonus) * g).astype(g.dtype)


def even_mixer(h, hc, lat_grid, ctx_grid, want_ctx, w_in, w_out, mu, w0, w_up, a0, a_up, g_up,
               k_k, k_a, r_k, gn_w, gn_b, pool_w, pool_scale):
    p = h @ w_in
    pc = hc @ w_in
    lat = rwkv_inputs(p[..., :D_RWKV_PROJ], mu, w0, w_up, a0, a_up, g_up, k_k, k_a)
    cx = rwkv_inputs(pc[..., :D_RWKV_PROJ], mu, w0, w_up, a0, a_up, g_up, k_k, k_a)
    zero = jnp.zeros((h.shape[0], N_RWKV_HEADS, HEAD_DIM, HEAD_DIM), jnp.float32)
    ys_l, ys_c = [], []
    for d in range(2):
        s_c, y_c = wkv_scan(zero, cx[0], cx[1], cx[2], cx[3][:, :, d], cx[4][:, :, d], cx[5][:, :, d], d == 1)
        _, y_l = wkv_scan(s_c, lat[0], lat[1], lat[2], lat[3][:, :, d], lat[4][:, :, d], lat[5][:, :, d], d == 1)
        ys_l.append(y_l)
        ys_c.append(y_c)

    def merge(inp, ys, pp, grid):
        r, v, _, _, kd, _, g = inp
        y_rwkv = rwkv_output(ys[0] + ys[1], r, v, kd, r_k, gn_w, gn_b, g)
        y_pool = multiscale_pool(pp[..., D_RWKV_PROJ:], pool_w, pool_scale, grid)
        return jnp.concatenate([y_rwkv, y_pool], axis=-1) @ w_out

    y = merge(lat, ys_l, p, lat_grid)
    y_ctx = merge(cx, ys_c, pc, ctx_grid) if want_ctx else None
    return y, y_ctx


def conv_mixer(h, w_in, conv_w, w_out, grid):
    p = h @ w_in
    bg, cg, u = jnp.split(p, 3, axis=-1)
    return (bg * dwconv3(cg * u, conv_w, grid)) @ w_out


def conv_ffn(h, w_up, conv_w, w_down, grid):
    p = h @ w_up
    return (jax.nn.silu(dwconv3(p[..., :D_FF], conv_w, grid)) * p[..., D_FF:]) @ w_down


def setup_inputs(seed: int = 0) -> dict:
    key = jax.random.key(seed)
    ks = iter(jax.random.split(key, 40))
    f32 = jnp.float32
    nrm = lambda shape, s: jax.random.normal(next(ks), shape, f32) * s
    uni = lambda shape, lo, hi: jax.random.uniform(next(ks), shape, f32, lo, hi)
    D = D_MODEL
    ne = (DEPTH + 1) // 2
    no = DEPTH // 2
    return {
        'x': nrm((BATCH, SEQ, D), 1.0),
        'c': nrm((BATCH, D), 1.0),
        'ctx': nrm((BATCH, CTX_LEN, D), 1.0),
        'c_ctx': nrm((D,), 1.0),
        'w_mod': nrm((DEPTH, D, 6 * D), D ** -0.5),
        'b_mod': nrm((DEPTH, 6 * D), 0.02),
        'norm_g': 1.0 + nrm((DEPTH, 4, D), 0.05),
        'ffn_w_up': nrm((DEPTH, D, 2 * D_FF), D ** -0.5),
        'ffn_conv': nrm((DEPTH, 3, D_FF), 3 ** -0.5),
        'ffn_w_down': nrm((DEPTH, D_FF, D), D_FF ** -0.5),
        'ev_w_in': nrm((ne, D, D_EVEN_IN), D ** -0.5),
        'ev_w_out': nrm((ne, D, D), D ** -0.5),
        'ev_mu': uni((ne, 2, D_RWKV_PROJ), 0.0, 0.5),
        'ev_w0': uni((ne, 2, D_RWKV), -4.0, 0.0),
        'ev_w_up': nrm((ne, 2, D_DECAY_LORA, D_RWKV), 0.5 * D_DECAY_LORA ** -0.5),
        'ev_a0': nrm((ne, 2, D_RWKV), 0.5),
        'ev_a_up': nrm((ne, 2, D_AAA_LORA, D_RWKV), D_AAA_LORA ** -0.5),
        'ev_g_up': nrm((ne, D_GATE_LORA, D_RWKV), D_GATE_LORA ** -0.5),
        'ev_k_k': 0.85 + nrm((ne, D_RWKV), 0.05),
        'ev_k_a': 1.0 + nrm((ne, D_RWKV), 0.05),
        'ev_r_k': nrm((ne, N_RWKV_HEADS, HEAD_DIM), 0.1),
        'ev_gn_w': 1.0 + nrm((ne, D_RWKV), 0.05),
        'ev_gn_b': nrm((ne, D_RWKV), 0.02),
        'ev_pool_w': nrm((ne, len(POOL_WINDOWS), POOL_GROUP, POOL_GROUP), POOL_GROUP ** -0.5),
        'ev_pool_scale': 1.0 + nrm((ne, D_POOL), 0.05),
        'od_w_in': nrm((no, D, 3 * D), D ** -0.5),
        'od_conv': nrm((no, 3, D), 3 ** -0.5),
        'od_w_out': nrm((no, D, D), D ** -0.5),
    }


def reference(x, c, ctx, c_ctx, w_mod, b_mod, norm_g, ffn_w_up, ffn_conv, ffn_w_down,
              ev_w_in, ev_w_out, ev_mu, ev_w0, ev_w_up, ev_a0, ev_a_up, ev_g_up, ev_k_k, ev_k_a,
              ev_r_k, ev_gn_w, ev_gn_b, ev_pool_w, ev_pool_scale, od_w_in, od_conv, od_w_out):
    rows = x.shape[1] // GRID_W
    lat_grid = (rows, GRID_W)
    ctx_grid = (1, ctx.shape[1])
    for layer in range(DEPTH):
        i = layer // 2
        even = layer % 2 == 0
        ctx_later = any(j % 2 == 0 for j in range(layer + 1, DEPTH))
        mod = [m[:, None, :] for m in jnp.split(jax.nn.silu(c) @ w_mod[layer] + b_mod[layer], 6, axis=-1)]
        gn = norm_g[layer]
        if even or ctx_later:
            mod_c = jnp.split(jax.nn.silu(c_ctx) @ w_mod[layer] + b_mod[layer], 6, axis=-1)
            hc = modulate(rmsnorm(ctx, gn[0]), mod_c[0], mod_c[1])
        h = modulate(rmsnorm(x, gn[0]), mod[0], mod[1])
        if even:
            y, y_ctx = even_mixer(h, hc, lat_grid, ctx_grid, ctx_later, ev_w_in[i], ev_w_out[i], ev_mu[i],
                                  ev_w0[i], ev_w_up[i], ev_a0[i], ev_a_up[i], ev_g_up[i], ev_k_k[i],
                                  ev_k_a[i], ev_r_k[i], ev_gn_w[i], ev_gn_b[i], ev_pool_w[i], ev_pool_scale[i])
        else:
            y = conv_mixer(h, od_w_in[i], od_conv[i], od_w_out[i], lat_grid)
            y_ctx = conv_mixer(hc, od_w_in[i], od_conv[i], od_w_out[i], ctx_grid) if ctx_later else None
        x = x + mod[2] * rmsnorm(y, gn[1])
        h = modulate(rmsnorm(x, gn[2]), mod[3], mod[4])
        x = x + mod[5] * rmsnorm(conv_ffn(h, ffn_w_up[layer], ffn_conv[layer], ffn_w_down[layer], lat_grid), gn[3])
        if ctx_later:
            ctx = ctx + mod_c[2] * rmsnorm(y_ctx, gn[1])
            hc = modulate(rmsnorm(ctx, gn[2]), mod_c[3], mod_c[4])
            ctx = ctx + mod_c[5] * rmsnorm(conv_ffn(hc, ffn_w_up[layer], ffn_conv[layer], ffn_w_down[layer], ctx_grid), gn[3])
    return x
```

```python
import functools

import numpy as np
import jax
import jax.numpy as jnp
from jax import lax
from jax.experimental import pallas as pl
from jax.experimental.pallas import tpu as pltpu

D_MODEL = 1024
DEPTH = 4
GRID_W = 64
HEAD_DIM = 64
N_HEADS = 8
D_RWKV = N_HEADS * HEAD_DIM
D_POOL = D_MODEL - D_RWKV
POOL_WINDOWS = (2, 4, 8, 16)
POOL_GROUP = D_POOL // len(POOL_WINDOWS)
D_DECAY_LORA = 32
D_AAA_LORA = 64
D_GATE_LORA = 96
D_FF = 2816
RMS_EPS = 1e-6
GN_EPS = 64e-5

C_LW = 3 * D_RWKV
C_LA = C_LW + 128
C_LG = C_LA + 128
C_SHIFT = C_LG + 128
C_POOL = C_SHIFT
C_EVEN = C_POOL + D_POOL

HALO = 8
CHUNK = 64
TM = 256

F32 = jnp.float32
BF16 = jnp.bfloat16
HIGHEST = lax.Precision.HIGHEST
VMEM_LIMIT = 48 * 1024 * 1024


def _sigmoid(x):
    return 1.0 / (1.0 + jnp.exp(-x))


def _softplus(x):
    return jnp.maximum(x, 0.0) + jnp.log(1.0 + jnp.exp(-jnp.abs(x)))


def _rms(x, g):
    ms = jnp.mean(x * x, axis=-1, keepdims=True)
    return x * lax.rsqrt(ms + RMS_EPS) * g


def _norm_mod(x, g, shift, scale):
    return _rms(x, g) * (1.0 + scale) + shift


def _bdot(a, b):
    return jnp.dot(a.astype(BF16), b.astype(BF16), preferred_element_type=F32)


def _split_dot_rhs01(x, e):
    hi = x.astype(BF16)
    lo = (x - hi.astype(F32)).astype(BF16)
    return (jnp.dot(hi, e, preferred_element_type=F32)
            + jnp.dot(lo, e, preferred_element_type=F32))


def _split_dot_lhs01(e, x):
    hi = x.astype(BF16)
    lo = (x - hi.astype(F32)).astype(BF16)
    return (jnp.dot(e, hi, preferred_element_type=F32)
            + jnp.dot(e, lo, preferred_element_type=F32))


def _row_edges(tm, row_len):
    pos = lax.broadcasted_iota(jnp.int32, (tm, 1), 0) & (row_len - 1)
    return pos == 0, pos == row_len - 1


def _dwconv3(x, w_ref, first, last, tm):
    prev = jnp.where(first, 0.0, pltpu.roll(x, 1, 0))
    nxt = jnp.where(last, 0.0, pltpu.roll(x, tm - 1, 0))
    return prev * w_ref[0:1, :] + x * w_ref[1:2, :] + nxt * w_ref[2:3, :]


def _const_spec(shape):
    nd = len(shape)
    return pl.BlockSpec(shape, lambda *_: (0,) * nd, pipeline_mode=pl.Buffered(1))


def _params(n_axes):
    return pltpu.CompilerParams(dimension_semantics=("arbitrary",) * n_axes,
                                vmem_limit_bytes=VMEM_LIMIT)


def _mods_kernel(cv_ref, w_ref, b_ref, o_ref):
    cv = cv_ref[...]
    s = cv * _sigmoid(cv)
    o_ref[0] = jnp.dot(s, w_ref[0], preferred_element_type=F32, precision=HIGHEST) + b_ref[0]


def _mods(cv, w_mod, b_mod):
    tn = 1536
    n6 = 6 * D_MODEL
    return pl.pallas_call(
        _mods_kernel,
        grid=(DEPTH, n6 // tn),
        in_specs=[pl.BlockSpec((16, D_MODEL), lambda l, n: (0, 0)),
                  pl.BlockSpec((1, D_MODEL, tn), lambda l, n: (l, 0, n)),
                  pl.BlockSpec((1, 1, tn), lambda l, n: (l, 0, n))],
        out_specs=pl.BlockSpec((1, 16, tn), lambda l, n: (l, 0, n)),
        out_shape=jax.ShapeDtypeStruct((DEPTH, 16, n6), F32),
        compiler_params=_params(2),
        name="mods",
    )(cv, w_mod, b_mod.reshape(DEPTH, 1, n6))


def _even_in_kernel(x_ref, xp_ref, xn_ref, mod_ref, gn_ref, win_ref, mu_ref, wup_ref, aup_ref,
                    gup_ref, vec_ref, e_ref, sh_ref, dr_ref, mg_ref, *, tm, nt):
    j = pl.program_id(1)
    te = tm + 2 * HALO
    xe = jnp.concatenate([xp_ref[0], x_ref[0], xn_ref[0]], axis=0)
    h = _norm_mod(xe, gn_ref[0:1, :], mod_ref[0, 0:1, :], mod_ref[0, 1:2, :])
    p = jnp.dot(h.astype(BF16), win_ref[...], preferred_element_type=F32)
    pe = p[:, :C_SHIFT]
    rows = lax.broadcasted_iota(jnp.int32, (te, 1), 0)
    lo = jnp.where(j == 0, HALO, 0)
    hi = jnp.where(j == nt - 1, tm + HALO, te)
    pe = jnp.where(rows >= lo, pe, 0.0)
    pe = jnp.where(rows < hi, pe, 0.0)
    prev = pltpu.roll(pe, 1, 0)[HALO:tm + HALO]
    nxt = pltpu.roll(pe, te - 1, 0)[HALO:tm + HALO]
    pc = pe[HALO:tm + HALO]
    pr = pc + mu_ref[0:1, :] * (prev - pc) + mu_ref[1:2, :] * (nxt - pc)

    r = pr[:, 0:D_RWKV]
    k = pr[:, D_RWKV:2 * D_RWKV]
    v = pr[:, 2 * D_RWKV:3 * D_RWKV]
    wd = _bdot(jnp.tanh(pr[:, C_LW:C_LW + 128]), wup_ref[...])
    la = _bdot(pr[:, C_LA:C_LA + 128], aup_ref[...])
    g = _bdot(_sigmoid(pr[:, C_LG:C_LG + 128]), gup_ref[...])

    k_k = vec_ref[4:5, :]
    k_a = vec_ref[5:6, :]
    r_k = vec_ref[6:7, :]
    kkr = k * k_k
    nrm = _split_dot_rhs01(kkr * kkr, e_ref[...])
    kk = kkr * lax.rsqrt(jnp.maximum(nrm, 1e-24))
    kd_sum = None
    for d in range(2):
        wdd = vec_ref[d:d + 1, :] + wd[:, d * D_RWKV:(d + 1) * D_RWKV]
        w = -_softplus(-wdd) - 0.5
        ld = -jnp.exp(w)
        a = _sigmoid(vec_ref[2 + d:3 + d, :] + la[:, d * D_RWKV:(d + 1) * D_RWKV])
        kd = k * (1.0 + (a - 1.0) * k_a)
        dr_ref[d, 0, :, 0:D_RWKV] = ld
        dr_ref[d, 0, :, D_RWKV:2 * D_RWKV] = kd
        dr_ref[d, 0, :, 2 * D_RWKV:3 * D_RWKV] = kk * a
        kd_sum = kd if kd_sum is None else kd_sum + kd
    coef = _split_dot_rhs01(r * kd_sum * r_k, e_ref[...])
    sh_ref[0, :, 0:D_RWKV] = r
    sh_ref[0, :, D_RWKV:2 * D_RWKV] = v
    sh_ref[0, :, 2 * D_RWKV:3 * D_RWKV] = kk
    mg_ref[0, :, 0:D_RWKV] = coef * v
    mg_ref[0, :, D_RWKV:2 * D_RWKV] = g
    mg_ref[0, :, 2 * D_RWKV:3 * D_RWKV] = p[HALO:tm + HALO, C_POOL:C_EVEN]


def _even_in(x, mod, gn, ew):
    bsz, seq, _ = x.shape
    tm = min(TM, seq)
    nt = seq // tm
    hb = tm // HALO
    nhb = seq // HALO
    c3 = 3 * D_RWKV
    kern = functools.partial(_even_in_kernel, tm=tm, nt=nt)
    mod_map = (lambda b, j: (b, 0, 0)) if mod.shape[0] == bsz else (lambda b, j: (0, 0, 0))
    return pl.pallas_call(
        kern,
        grid=(bsz, nt),
        in_specs=[pl.BlockSpec((1, tm, D_MODEL), lambda b, j: (b, j, 0)),
                  pl.BlockSpec((1, HALO, D_MODEL), lambda b, j: (b, jnp.maximum(j * hb - 1, 0), 0)),
                  pl.BlockSpec((1, HALO, D_MODEL), lambda b, j: (b, jnp.minimum((j + 1) * hb, nhb - 1), 0)),
                  pl.BlockSpec((1, 6, D_MODEL), mod_map),
                  _const_spec((4, D_MODEL)),
                  _const_spec((D_MODEL, C_EVEN)),
                  _const_spec((2, C_SHIFT)),
                  _const_spec((128, 2 * D_RWKV)),
                  _const_spec((128, 2 * D_RWKV)),
                  _const_spec((128, D_RWKV)),
                  _const_spec((16, D_RWKV)),
                  _const_spec((D_RWKV, D_RWKV))],
        out_specs=[pl.BlockSpec((1, tm, c3), lambda b, j: (b, j, 0)),
                   pl.BlockSpec((2, 1, tm, c3), lambda b, j: (0, b, j, 0)),
                   pl.BlockSpec((1, tm, c3), lambda b, j: (b, j, 0))],
        out_shape=[jax.ShapeDtypeStruct((bsz, seq, c3), F32),
                   jax.ShapeDtypeStruct((2, bsz, seq, c3), F32),
                   jax.ShapeDtypeStruct((bsz, seq, c3), F32)],
        compiler_params=_params(2),
        name="even_in",
    )(x, x, x, mod, gn, ew["w_in"], ew["mu"], ew["wup"], ew["aup"], ew["gup"], ew["vec"], ew["seg"])


def _mm(a, b):
    return jnp.dot(a, b, preferred_element_type=F32, precision=HIGHEST)


def _mm_nt(a, b):
    return lax.dot_general(a, b, (((1,), (1,)), ((), ())), preferred_element_type=F32,
                           precision=HIGHEST)


def _mm_tn(a, b):
    return lax.dot_general(a, b, (((0,), (0,)), ((), ())), preferred_element_type=F32,
                           precision=HIGHEST)


def _wkv_chunk_dir(sh, dr, s_ref, d, y_ref, reverse, L):
    row = lax.broadcasted_iota(jnp.int32, (L, L), 0)
    col = lax.broadcasted_iota(jnp.int32, (L, L), 1)
    incl = (row <= col) if reverse else (row >= col)
    strict = (row < col) if reverse else (row > col)
    eye = (row == col).astype(F32)

    r = sh[:, 0:D_RWKV]
    v = sh[:, D_RWKV:2 * D_RWKV]
    kk = sh[:, 2 * D_RWKV:3 * D_RWKV]
    ld = dr[:, 0:D_RWKV]
    kd = dr[:, D_RWKV:2 * D_RWKV]
    b = dr[:, 2 * D_RWKV:3 * D_RWKV]

    cum = _mm(incl.astype(F32), ld)
    tot = jnp.sum(ld, axis=0, keepdims=True)
    kkt = kk * jnp.exp(cum - ld)
    rt = r * jnp.exp(cum)
    iw = jnp.exp(-cum)
    kh = kd * iw
    bh = b * iw
    wl = jnp.exp(tot - cum)
    khw = kd * wl
    bhw = b * wl
    wtot = jnp.exp(tot)

    for h in range(N_HEADS):
        sl = slice(h * HEAD_DIM, (h + 1) * HEAD_DIM)
        kkt_h, rt_h, kh_h, bh_h = kkt[:, sl], rt[:, sl], kh[:, sl], bh[:, sl]
        v_h, khw_h, bhw_h = v[:, sl], khw[:, sl], bhw[:, sl]
        lhs = jnp.concatenate([kkt_h, rt_h], axis=0)
        sc_b = _mm_nt(lhs, bh_h)
        sc_k = _mm_nt(lhs, kh_h)
        a_ab = jnp.where(strict, sc_b[:L], 0.0)
        a_ak = jnp.where(strict, sc_k[:L], 0.0)
        a_rb = jnp.where(incl, sc_b[L:], 0.0)
        a_rk = jnp.where(incl, sc_k[L:], 0.0)
        xp = -a_ab
        t = eye + xp
        n = 2
        while n < L:
            xp = _mm(xp, xp)
            t = t + _mm(xp, t)
            n *= 2
        av = _mm(a_ak, v_h)
        pm = _mm(t, kkt_h)
        u0 = _mm(t, av)
        q = rt_h - _mm(a_rb, pm)
        y0 = _mm(a_rk, v_h) - _mm(a_rb, u0)
        mc = _mm_tn(pm, bhw_h)
        gg = _mm_tn(v_h, khw_h) - _mm_tn(u0, bhw_h)
        s0 = s_ref[d, h]
        y_ref[0, :, sl] = _mm_nt(q, s0) + y0
        s_ref[d, h] = s0 * wtot[:, sl] - _mm(s0, mc) + gg


def _wkv_kernel(shf_ref, shb_ref, drf_ref, drb_ref, s0_ref, yf_ref, yb_ref, sout_ref, s_ref,
                *, L, nc):
    j = pl.program_id(1)

    @pl.when(j == 0)
    def _():
        s_ref[...] = s0_ref[0]

    _wkv_chunk_dir(shf_ref[0], drf_ref[0, 0], s_ref, 0, yf_ref, False, L)
    _wkv_chunk_dir(shb_ref[0], drb_ref[0, 0], s_ref, 1, yb_ref, True, L)

    @pl.when(j == nc - 1)
    def _():
        sout_ref[0] = s_ref[...]


def _wkv(sh, dr, s0):
    bsz, seq, c3 = sh.shape
    L = CHUNK
    nc = seq // L
    kern = functools.partial(_wkv_kernel, L=L, nc=nc)
    st_shape = (1, 2, N_HEADS, HEAD_DIM, HEAD_DIM)
    return pl.pallas_call(
        kern,
        grid=(bsz, nc),
        in_specs=[pl.BlockSpec((1, L, c3), lambda b, j: (b, j, 0)),
                  pl.BlockSpec((1, L, c3), lambda b, j: (b, nc - 1 - j, 0)),
                  pl.BlockSpec((1, 1, L, c3), lambda b, j: (0, b, j, 0)),
                  pl.BlockSpec((1, 1, L, c3), lambda b, j: (1, b, nc - 1 - j, 0)),
                  pl.BlockSpec(st_shape, lambda b, j: (b, 0, 0, 0, 0))],
        out_specs=[pl.BlockSpec((1, L, D_RWKV), lambda b, j: (b, j, 0)),
                   pl.BlockSpec((1, L, D_RWKV), lambda b, j: (b, nc - 1 - j, 0)),
                   pl.BlockSpec(st_shape, lambda b, j: (b, 0, 0, 0, 0))],
        out_shape=[jax.ShapeDtypeStruct((bsz, seq, D_RWKV), F32),
                   jax.ShapeDtypeStruct((bsz, seq, D_RWKV), F32),
                   jax.ShapeDtypeStruct((bsz, 2, N_HEADS, HEAD_DIM, HEAD_DIM), F32)],
        scratch_shapes=[pltpu.VMEM((2, N_HEADS, HEAD_DIM, HEAD_DIM), F32)],
        compiler_params=_params(2),
        name="wkv",
    )(sh, sh, dr, dr, s0)


def _even_out_kernel(x_ref, yf_ref, yb_ref, mg_ref, mod_ref, gn_ref, vec_ref, e_ref, band_ref,
                     icnt_ref, pw_ref, wout_ref, o_ref):
    y = yf_ref[0] + yb_ref[0]
    inv_n = 1.0 / HEAD_DIM
    mean = _split_dot_rhs01(y, e_ref[...]) * inv_n
    yc = y - mean
    var = _split_dot_rhs01(yc * yc, e_ref[...]) * inv_n
    yn = yc * lax.rsqrt(var + GN_EPS) * vec_ref[7:8, :] + vec_ref[8:9, :]
    y_rwkv = (yn + mg_ref[0, :, 0:D_RWKV]) * mg_ref[0, :, D_RWKV:2 * D_RWKV]

    pp = mg_ref[0, :, 2 * D_RWKV:3 * D_RWKV]
    outs = []
    for gi in range(len(POOL_WINDOWS)):
        sl = slice(gi * POOL_GROUP, (gi + 1) * POOL_GROUP)
        xg = pp[:, sl]
        wsum = _split_dot_lhs01(band_ref[gi], xg)
        dg = wsum * icnt_ref[:, sl] - xg
        outs.append(_bdot(dg, pw_ref[gi]))
    y_pool = jnp.concatenate(outs, axis=-1) * vec_ref[9:10, :]

    cat = jnp.concatenate([y_rwkv, y_pool], axis=-1)
    yo = _bdot(cat, wout_ref[...])
    o_ref[0] = x_ref[0] + mod_ref[0, 2:3, :] * _rms(yo, gn_ref[1:2, :])


def _even_out(x, yf, yb, mg, mod, gn, ew, pc):
    bsz, seq, _ = x.shape
    tm = min(TM, seq)
    nt = seq // tm
    c3 = 3 * D_RWKV
    mod_map = (lambda b, j: (b, 0, 0)) if mod.shape[0] == bsz else (lambda b, j: (0, 0, 0))
    tok = lambda w: pl.BlockSpec((1, tm, w), lambda b, j: (b, j, 0))
    return pl.pallas_call(
        _even_out_kernel,
        grid=(bsz, nt),
        in_specs=[tok(D_MODEL), tok(D_RWKV), tok(D_RWKV), tok(c3),
                  pl.BlockSpec((1, 6, D_MODEL), mod_map),
                  _const_spec((4, D_MODEL)),
                  _const_spec((16, D_RWKV)),
                  _const_spec((D_RWKV, D_RWKV)),
                  _const_spec((4, tm, tm)),
                  _const_spec((tm, D_POOL)),
                  _const_spec((4, POOL_GROUP, POOL_GROUP)),
                  _const_spec((D_MODEL, D_MODEL))],
        out_specs=tok(D_MODEL),
        out_shape=jax.ShapeDtypeStruct(x.shape, F32),
        compiler_params=_params(2),
        name="even_out",
    )(x, yf, yb, mg, mod, gn, ew["vec"], ew["seg"], pc["band"], pc["icnt"], ew["pool_w"], ew["w_out"])


def _odd_kernel(x_ref, mod_ref, gn_ref, win_ref, cw_ref, wout_ref, o_ref, *, tm, row_len):
    x = x_ref[0]
    h = _norm_mod(x, gn_ref[0:1, :], mod_ref[0, 0:1, :], mod_ref[0, 1:2, :])
    p = jnp.dot(h.astype(BF16), win_ref[...], preferred_element_type=F32)
    bg = p[:, 0:D_MODEL]
    cg = p[:, D_MODEL:2 * D_MODEL]
    u = p[:, 2 * D_MODEL:3 * D_MODEL]
    first, last = _row_edges(tm, row_len)
    z = bg * _dwconv3(cg * u, cw_ref, first, last, tm)
    yo = _bdot(z, wout_ref[...])
    o_ref[0] = x + mod_ref[0, 2:3, :] * _rms(yo, gn_ref[1:2, :])


def _odd_mix(x, mod, gn, ow, row_len):
    bsz, seq, _ = x.shape
    tm = min(TM, seq)
    nt = seq // tm
    assert tm % row_len == 0 and row_len & (row_len - 1) == 0
    mod_map = (lambda b, j: (b, 0, 0)) if mod.shape[0] == bsz else (lambda b, j: (0, 0, 0))
    tok = pl.BlockSpec((1, tm, D_MODEL), lambda b, j: (b, j, 0))
    return pl.pallas_call(
        functools.partial(_odd_kernel, tm=tm, row_len=row_len),
        grid=(bsz, nt),
        in_specs=[tok, pl.BlockSpec((1, 6, D_MODEL), mod_map),
                  _const_spec((4, D_MODEL)),
                  _const_spec((D_MODEL, 3 * D_MODEL)),
                  _const_spec((3, D_MODEL)),
                  _const_spec((D_MODEL, D_MODEL))],
        out_specs=tok,
        out_shape=jax.ShapeDtypeStruct(x.shape, F32),
        compiler_params=_params(2),
        name="odd_mix",
    )(x, mod, gn, ow["w_in"], ow["conv"], ow["w_out"])


def _ffn_kernel(x_ref, mod_ref, gn_ref, wup_ref, cw_ref, wdn_ref, o_ref, *, tm, row_len):
    x = x_ref[0]
    h = _norm_mod(x, gn_ref[2:3, :], mod_ref[0, 3:4, :], mod_ref[0, 4:5, :])
    p = jnp.dot(h.astype(BF16), wup_ref[...], preferred_element_type=F32)
    first, last = _row_edges(tm, row_len)
    c = _dwconv3(p[:, :D_FF], cw_ref, first, last, tm)
    u = (c * _sigmoid(c)) * p[:, D_FF:]
    yo = _bdot(u, wdn_ref[...])
    o_ref[0] = x + mod_ref[0, 5:6, :] * _rms(yo, gn_ref[3:4, :])


def _ffn(x, mod, gn, fw, row_len):
    bsz, seq, _ = x.shape
    tm = min(TM, seq)
    nt = seq // tm
    assert tm % row_len == 0 and row_len & (row_len - 1) == 0
    mod_map = (lambda b, j: (b, 0, 0)) if mod.shape[0] == bsz else (lambda b, j: (0, 0, 0))
    tok = pl.BlockSpec((1, tm, D_MODEL), lambda b, j: (b, j, 0))
    return pl.pallas_call(
        functools.partial(_ffn_kernel, tm=tm, row_len=row_len),
        grid=(bsz, nt),
        in_specs=[tok, pl.BlockSpec((1, 6, D_MODEL), mod_map),
                  _const_spec((4, D_MODEL)),
                  _const_spec((D_MODEL, 2 * D_FF)),
                  _const_spec((3, D_FF)),
                  _const_spec((D_FF, D_MODEL))],
        out_specs=tok,
        out_shape=jax.ShapeDtypeStruct(x.shape, F32),
        compiler_params=_params(2),
        name="ffn",
    )(x, mod, gn, fw["w_up"], fw["conv"], fw["w_down"])


def _pool_consts(tm, row_len):
    pos = np.arange(tm)
    col = pos % row_len
    same_row = (pos[:, None] // row_len) == (pos[None, :] // row_len)
    band = np.zeros((len(POOL_WINDOWS), tm, tm), np.float32)
    icnt = np.zeros((tm, D_POOL), np.float32)
    for gi, win in enumerate(POOL_WINDOWS):
        lo = np.clip(col - win // 2, 0, row_len)
        hi = np.clip(col + win // 2, 0, row_len)
        inside = (col[None, :] >= lo[:, None]) & (col[None, :] < hi[:, None]) & same_row
        band[gi] = inside.astype(np.float32)
        icnt[:, gi * POOL_GROUP:(gi + 1) * POOL_GROUP] = (1.0 / (hi - lo).astype(np.float32))[:, None]
    return {"band": jnp.asarray(band, BF16), "icnt": jnp.asarray(icnt, F32)}


def _even_weights(i, ev_w_in, ev_w_out, ev_mu, ev_w0, ev_w_up, ev_a0, ev_a_up, ev_g_up, ev_k_k,
                  ev_k_a, ev_r_k, ev_gn_w, ev_gn_b, ev_pool_w, ev_pool_scale):
    o_lw = 3 * D_RWKV
    o_la = o_lw + 2 * D_DECAY_LORA
    o_lg = o_la + 2 * D_AAA_LORA
    o_pool = o_lg + D_GATE_LORA

    def repack(a, with_pool):
        z = lambda n: jnp.zeros(a.shape[:-1] + (n,), a.dtype)
        parts = [a[..., :o_lw], a[..., o_lw:o_la], z(128 - 2 * D_DECAY_LORA),
                 a[..., o_la:o_lg], a[..., o_lg:o_pool], z(128 - D_GATE_LORA)]
        if with_pool:
            parts.append(a[..., o_pool:])
        return jnp.concatenate(parts, axis=-1)

    wup = jnp.zeros((128, 2 * D_RWKV), F32)
    aup = jnp.zeros((128, 2 * D_RWKV), F32)
    for d in range(2):
        wup = wup.at[d * D_DECAY_LORA:(d + 1) * D_DECAY_LORA, d * D_RWKV:(d + 1) * D_RWKV].set(ev_w_up[i, d])
        aup = aup.at[d * D_AAA_LORA:(d + 1) * D_AAA_LORA, d * D_RWKV:(d + 1) * D_RWKV].set(ev_a_up[i, d])
    gup = jnp.zeros((128, D_RWKV), F32).at[:D_GATE_LORA].set(ev_g_up[i])
    vec = jnp.zeros((16, D_RWKV), F32)
    vec = vec.at[0:2].set(ev_w0[i]).at[2:4].set(ev_a0[i]).at[4].set(ev_k_k[i]).at[5].set(ev_k_a[i])
    vec = vec.at[6].set(ev_r_k[i].reshape(D_RWKV)).at[7].set(ev_gn_w[i]).at[8].set(ev_gn_b[i])
    vec = vec.at[9].set(ev_pool_scale[i])
    head = np.arange(D_RWKV) // HEAD_DIM
    seg = jnp.asarray(head[:, None] == head[None, :], BF16)
    return {"w_in": repack(ev_w_in[i], True).astype(BF16), "mu": repack(ev_mu[i], False),
            "wup": wup.astype(BF16), "aup": aup.astype(BF16), "gup": gup.astype(BF16),
            "vec": vec, "seg": seg, "pool_w": ev_pool_w[i].astype(BF16),
            "w_out": ev_w_out[i].astype(BF16)}


def kernel(x, c, ctx, c_ctx, w_mod, b_mod, norm_g, ffn_w_up, ffn_conv, ffn_w_down, ev_w_in, ev_w_out,
           ev_mu, ev_w0, ev_w_up, ev_a0, ev_a_up, ev_g_up, ev_k_k, ev_k_a, ev_r_k, ev_gn_w, ev_gn_b,
           ev_pool_w, ev_pool_scale, od_w_in, od_conv, od_w_out):
    bsz, seq, _ = x.shape
    ctx_len = ctx.shape[1]
    assert seq % TM == 0 and TM % GRID_W == 0 and ctx_len <= TM and ctx_len % CHUNK == 0

    cv = jnp.zeros((16, D_MODEL), F32).at[:bsz].set(c).at[bsz].set(c_ctx)
    mods = _mods(cv, w_mod, b_mod)
    pc_lat = _pool_consts(TM, GRID_W)
    pc_ctx = _pool_consts(ctx_len, ctx_len)

    for layer in range(DEPTH):
        i = layer // 2
        even = layer % 2 == 0
        ctx_later = any(jj % 2 == 0 for jj in range(layer + 1, DEPTH))
        mod = mods[layer, :bsz].reshape(bsz, 6, D_MODEL)
        mod_c = mods[layer, bsz:bsz + 1].reshape(1, 6, D_MODEL)
        gn = norm_g[layer]
        fw = {"w_up": ffn_w_up[layer].astype(BF16), "conv": ffn_conv[layer],
              "w_down": ffn_w_down[layer].astype(BF16)}
        if even:
            ew = _even_weights(i, ev_w_in, ev_w_out, ev_mu, ev_w0, ev_w_up, ev_a0, ev_a_up, ev_g_up,
                               ev_k_k, ev_k_a, ev_r_k, ev_gn_w, ev_gn_b, ev_pool_w, ev_pool_scale)
            sh_c, dr_c, mg_c = _even_in(ctx, mod_c, gn, ew)
            sh_l, dr_l, mg_l = _even_in(x, mod, gn, ew)
            zero = jnp.zeros((bsz, 2, N_HEADS, HEAD_DIM, HEAD_DIM), F32)
            yf_c, yb_c, s_c = _wkv(sh_c, dr_c, zero)
            yf_l, yb_l, _ = _wkv(sh_l, dr_l, s_c)
            x_new = _even_out(x, yf_l, yb_l, mg_l, mod, gn, ew, pc_lat)
            if ctx_later:
                ctx = _even_out(ctx, yf_c, yb_c, mg_c, mod_c, gn, ew, pc_ctx)
            x = x_new
        else:
            ow = {"w_in": od_w_in[i].astype(BF16), "conv": od_conv[i], "w_out": od_w_out[i].astype(BF16)}
            x = _odd_mix(x, mod, gn, ow, GRID_W)
            if ctx_later:
                ctx = _odd_mix(ctx, mod_c, gn, ow, ctx_len)
        x = _ffn(x, mod, gn, fw, GRID_W)
        if ctx_later:
            ctx = _ffn(ctx, mod_c, gn, fw, ctx_len)
    return x
```

```python
import functools

import numpy as np
import jax
import jax.numpy as jnp
from jax import lax
from jax.experimental import pallas as pl
from jax.experimental.pallas import tpu as pltpu

D_MODEL = 1024
DEPTH = 4
GRID_W = 64
HEAD_DIM = 64
N_HEADS = 8
D_RWKV = N_HEADS * HEAD_DIM
D_POOL = D_MODEL - D_RWKV
POOL_WINDOWS = (2, 4, 8, 16)
POOL_GROUP = D_POOL // len(POOL_WINDOWS)
D_DECAY_LORA = 32
D_AAA_LORA = 64
D_GATE_LORA = 96
D_FF = 2816
RMS_EPS = 1e-6
GN_EPS = 64e-5

C_LW = 3 * D_RWKV
C_LA = C_LW + 128
C_LG = C_LA + 128
C_SHIFT = C_LG + 128
C_POOL = C_SHIFT
C_EVEN = C_POOL + D_POOL

HALO = 8
CHUNK = 64
CHUNKS_PER_STEP = 2
TM = 256

F32 = jnp.float32
BF16 = jnp.bfloat16
HIGHEST = lax.Precision.HIGHEST
VMEM_LIMIT = 48 * 1024 * 1024


def _sigmoid(x):
    return 1.0 / (1.0 + jnp.exp(-x))


def _softplus(x):
    return jnp.maximum(x, 0.0) + jnp.log(1.0 + jnp.exp(-jnp.abs(x)))


def _rms(x, g):
    ms = jnp.mean(x * x, axis=-1, keepdims=True)
    return x * lax.rsqrt(ms + RMS_EPS) * g


def _norm_mod(x, g, shift, scale):
    return _rms(x, g) * (1.0 + scale) + shift


def _bdot(a, b):
    return jnp.dot(a.astype(BF16), b.astype(BF16), preferred_element_type=F32)


def _split_dot_rhs01(x, e):
    hi = x.astype(BF16)
    lo = (x - hi.astype(F32)).astype(BF16)
    return (jnp.dot(hi, e, preferred_element_type=F32)
            + jnp.dot(lo, e, preferred_element_type=F32))


def _split_dot_lhs01(e, x):
    hi = x.astype(BF16)
    lo = (x - hi.astype(F32)).astype(BF16)
    return (jnp.dot(e, hi, preferred_element_type=F32)
            + jnp.dot(e, lo, preferred_element_type=F32))


def _row_edges(tm, row_len):
    pos = lax.broadcasted_iota(jnp.int32, (tm, 1), 0) & (row_len - 1)
    return pos == 0, pos == row_len - 1


def _dwconv3(x, w_ref, first, last, tm):
    prev = jnp.where(first, 0.0, pltpu.roll(x, 1, 0))
    nxt = jnp.where(last, 0.0, pltpu.roll(x, tm - 1, 0))
    return prev * w_ref[0:1, :] + x * w_ref[1:2, :] + nxt * w_ref[2:3, :]


def _const_spec(shape):
    nd = len(shape)
    return pl.BlockSpec(shape, lambda *_: (0,) * nd, pipeline_mode=pl.Buffered(1))


def _params(n_axes):
    return pltpu.CompilerParams(dimension_semantics=("arbitrary",) * n_axes,
                                vmem_limit_bytes=VMEM_LIMIT)


def _mods_kernel(cv_ref, w_ref, b_ref, o_ref):
    cv = cv_ref[...]
    s = cv * _sigmoid(cv)
    o_ref[0] = jnp.dot(s, w_ref[0], preferred_element_type=F32, precision=HIGHEST) + b_ref[0]


def _mods(cv, w_mod, b_mod):
    tn = 1536
    n6 = 6 * D_MODEL
    return pl.pallas_call(
        _mods_kernel,
        grid=(DEPTH, n6 // tn),
        in_specs=[pl.BlockSpec((16, D_MODEL), lambda l, n: (0, 0)),
                  pl.BlockSpec((1, D_MODEL, tn), lambda l, n: (l, 0, n)),
                  pl.BlockSpec((1, 1, tn), lambda l, n: (l, 0, n))],
        out_specs=pl.BlockSpec((1, 16, tn), lambda l, n: (l, 0, n)),
        out_shape=jax.ShapeDtypeStruct((DEPTH, 16, n6), F32),
        compiler_params=_params(2),
        name="mods",
    )(cv, w_mod, b_mod.reshape(DEPTH, 1, n6))


def _even_in_kernel(x_ref, xp_ref, xn_ref, mod_ref, gn_ref, win_ref, mu_ref, wup_ref, aup_ref,
                    gup_ref, vec_ref, e_ref, sh_ref, dr_ref, mg_ref, *, tm, nt):
    j = pl.program_id(1)
    te = tm + 2 * HALO
    xe = jnp.concatenate([xp_ref[0], x_ref[0], xn_ref[0]], axis=0)
    h = _norm_mod(xe, gn_ref[0:1, :], mod_ref[0, 0:1, :], mod_ref[0, 1:2, :])
    p = jnp.dot(h.astype(BF16), win_ref[...], preferred_element_type=F32)
    pe = p[:, :C_SHIFT]
    rows = lax.broadcasted_iota(jnp.int32, (te, 1), 0)
    lo = jnp.where(j == 0, HALO, 0)
    hi = jnp.where(j == nt - 1, tm + HALO, te)
    pe = jnp.where(rows >= lo, pe, 0.0)
    pe = jnp.where(rows < hi, pe, 0.0)
    prev = pltpu.roll(pe, 1, 0)[HALO:tm + HALO]
    nxt = pltpu.roll(pe, te - 1, 0)[HALO:tm + HALO]
    pc = pe[HALO:tm + HALO]
    pr = pc + mu_ref[0:1, :] * (prev - pc) + mu_ref[1:2, :] * (nxt - pc)

    r = pr[:, 0:D_RWKV]
    k = pr[:, D_RWKV:2 * D_RWKV]
    v = pr[:, 2 * D_RWKV:3 * D_RWKV]
    wd = _bdot(jnp.tanh(pr[:, C_LW:C_LW + 128]), wup_ref[...])
    la = _bdot(pr[:, C_LA:C_LA + 128], aup_ref[...])
    g = _bdot(_sigmoid(pr[:, C_LG:C_LG + 128]), gup_ref[...])

    k_k = vec_ref[4:5, :]
    k_a = vec_ref[5:6, :]
    r_k = vec_ref[6:7, :]
    kkr = k * k_k
    nrm = _split_dot_rhs01(kkr * kkr, e_ref[...])
    kk = kkr * lax.rsqrt(jnp.maximum(nrm, 1e-24))
    kd_sum = None
    for d in range(2):
        wdd = vec_ref[d:d + 1, :] + wd[:, d * D_RWKV:(d + 1) * D_RWKV]
        w = -_softplus(-wdd) - 0.5
        ld = -jnp.exp(w)
        a = _sigmoid(vec_ref[2 + d:3 + d, :] + la[:, d * D_RWKV:(d + 1) * D_RWKV])
        kd = k * (1.0 + (a - 1.0) * k_a)
        dr_ref[d, 0, :, 0:D_RWKV] = ld
        dr_ref[d, 0, :, D_RWKV:2 * D_RWKV] = kd
        dr_ref[d, 0, :, 2 * D_RWKV:3 * D_RWKV] = kk * a
        kd_sum = kd if kd_sum is None else kd_sum + kd
    coef = _split_dot_rhs01(r * kd_sum * r_k, e_ref[...])
    sh_ref[0, :, 0:D_RWKV] = r
    sh_ref[0, :, D_RWKV:2 * D_RWKV] = v
    sh_ref[0, :, 2 * D_RWKV:3 * D_RWKV] = kk
    mg_ref[0, :, 0:D_RWKV] = coef * v
    mg_ref[0, :, D_RWKV:2 * D_RWKV] = g
    mg_ref[0, :, 2 * D_RWKV:3 * D_RWKV] = p[HALO:tm + HALO, C_POOL:C_EVEN]


def _even_in(x, mod, gn, ew):
    bsz, seq, _ = x.shape
    tm = min(TM, seq)
    nt = seq // tm
    hb = tm // HALO
    nhb = seq // HALO
    c3 = 3 * D_RWKV
    kern = functools.partial(_even_in_kernel, tm=tm, nt=nt)
    mod_map = (lambda b, j: (b, 0, 0)) if mod.shape[0] == bsz else (lambda b, j: (0, 0, 0))
    return pl.pallas_call(
        kern,
        grid=(bsz, nt),
        in_specs=[pl.BlockSpec((1, tm, D_MODEL), lambda b, j: (b, j, 0)),
                  pl.BlockSpec((1, HALO, D_MODEL), lambda b, j: (b, jnp.maximum(j * hb - 1, 0), 0)),
                  pl.BlockSpec((1, HALO, D_MODEL), lambda b, j: (b, jnp.minimum((j + 1) * hb, nhb - 1), 0)),
                  pl.BlockSpec((1, 6, D_MODEL), mod_map),
                  _const_spec((4, D_MODEL)),
                  _const_spec((D_MODEL, C_EVEN)),
                  _const_spec((2, C_SHIFT)),
                  _const_spec((128, 2 * D_RWKV)),
                  _const_spec((128, 2 * D_RWKV)),
                  _const_spec((128, D_RWKV)),
                  _const_spec((16, D_RWKV)),
                  _const_spec((D_RWKV, D_RWKV))],
        out_specs=[pl.BlockSpec((1, tm, c3), lambda b, j: (b, j, 0)),
                   pl.BlockSpec((2, 1, tm, c3), lambda b, j: (0, b, j, 0)),
                   pl.BlockSpec((1, tm, c3), lambda b, j: (b, j, 0))],
        out_shape=[jax.ShapeDtypeStruct((bsz, seq, c3), F32),
                   jax.ShapeDtypeStruct((2, bsz, seq, c3), F32),
                   jax.ShapeDtypeStruct((bsz, seq, c3), F32)],
        compiler_params=_params(2),
        name="even_in",
    )(x, x, x, mod, gn, ew["w_in"], ew["mu"], ew["wup"], ew["aup"], ew["gup"], ew["vec"], ew["seg"])


LANES = 128
N_PAIRS = N_HEADS // 2


def _mm(a, b):
    return jnp.dot(a.astype(BF16), b.astype(BF16), preferred_element_type=F32)


def _mm_nt(a, b):
    return lax.dot_general(a.astype(BF16), b.astype(BF16), (((1,), (1,)), ((), ())),
                           preferred_element_type=F32)


def _mm_tn(a, b):
    return lax.dot_general(a.astype(BF16), b.astype(BF16), (((0,), (0,)), ((), ())),
                           preferred_element_type=F32)


def _wkv_masks(reverse, L):
    row = lax.broadcasted_iota(jnp.int32, (L, LANES), 0)
    lane = lax.broadcasted_iota(jnp.int32, (L, LANES), 1)
    colp = lane & (HEAD_DIM - 1)
    row2 = lax.broadcasted_iota(jnp.int32, (2 * L, LANES), 0)
    lane2 = lax.broadcasted_iota(jnp.int32, (2 * L, LANES), 1)
    rowt = lax.broadcasted_iota(jnp.int32, (L, L), 0)
    colt = lax.broadcasted_iota(jnp.int32, (L, L), 1)
    blk = row ^ colp
    m = {
        "left": lane < HEAD_DIM,
        "incl": (row <= colp) if reverse else (row >= colp),
        "strict": (row < colp) if reverse else (row > colp),
        "eye": (row == colp).astype(F32),
        "diag_blocks": (row2 < L) == (lane2 < HEAD_DIM),
        "tri": ((rowt <= colt) if reverse else (rowt >= colt)).astype(F32),
        "same2": (blk >> 1) == 0,
    }
    g, lg = 2, 1
    while g < L:
        m["off%d" % g] = (blk >> lg) == 1
        g, lg = 2 * g, lg + 1
    return m


def _wkv_precompute(sh, dr, m, L):
    left = m["left"]

    def bd(x):
        return jnp.concatenate([jnp.where(left, x, 0.0), jnp.where(left, 0.0, x)], axis=0)

    def pd(y):
        return jnp.where(left, y[:L], y[L:])

    r = sh[:, 0:D_RWKV]
    v = sh[:, D_RWKV:2 * D_RWKV]
    kk = sh[:, 2 * D_RWKV:3 * D_RWKV]
    ld = dr[:, 0:D_RWKV]
    kd = dr[:, D_RWKV:2 * D_RWKV]
    b = dr[:, 2 * D_RWKV:3 * D_RWKV]

    cum = jnp.dot(m["tri"], ld, preferred_element_type=F32, precision=HIGHEST)
    tot = jnp.sum(ld, axis=0, keepdims=True)
    kkt = kk * jnp.exp(cum - ld)
    rt = r * jnp.exp(cum)
    iw = jnp.exp(-cum)
    kh = kd * iw
    bh = b * iw
    wl = jnp.exp(tot - cum)
    khw = kd * wl
    bhw = b * wl
    wtot = jnp.exp(tot)

    def pair_chain(p, res):
        sl = slice(p * LANES, (p + 1) * LANES)
        kkt_p, rt_p, kh_p, bh_p = kkt[:, sl], rt[:, sl], kh[:, sl], bh[:, sl]
        v_p, khw_p, bhw_p = v[:, sl], khw[:, sl], bhw[:, sl]
        lhs = jnp.concatenate([kkt_p, rt_p], axis=0)
        sb = _mm_nt(lhs, bd(bh_p))
        sk = _mm_nt(lhs, bd(kh_p))
        yield
        a_ab = jnp.where(m["strict"], sb[:L], 0.0)
        a_rb = jnp.where(m["incl"], sb[L:], 0.0)
        a_ak = jnp.where(m["strict"], sk[:L], 0.0)
        a_rk = jnp.where(m["incl"], sk[L:], 0.0)
        avk = _mm(jnp.concatenate([a_ak, a_rk], axis=0), bd(v_p))
        t = m["eye"] - jnp.where(m["same2"], a_ab, 0.0)
        g = 2
        while g < L:
            w = _mm(t, bd(jnp.where(m["off%d" % g], a_ab, 0.0)))
            yield
            t = t - _mm(w, bd(t))
            yield
            g *= 2
        pu = _mm(t, jnp.concatenate([bd(kkt_p), bd(avk[:L])], axis=1))
        yield
        pm = pu[:, :LANES]
        u0 = pu[:, LANES:]
        qy = _mm(a_rb, jnp.concatenate([bd(pm), bd(u0)], axis=1))
        mc = jnp.where(m["diag_blocks"], _mm_tn(pm, bhw_p), 0.0)
        gg = pd(_mm_tn(jnp.concatenate([v_p, u0], axis=0),
                       jnp.concatenate([khw_p, -bhw_p], axis=0)))
        q = rt_p - qy[:, :LANES]
        y0 = avk[L:] - qy[:, LANES:]
        res[p] = (q, y0, mc, gg, wtot[:, sl])

    res = [None] * N_PAIRS
    return res, [pair_chain(p, res) for p in range(N_PAIRS)]


def _round_robin(chains):
    chains = list(chains)
    while chains:
        alive = []
        for ch in chains:
            try:
                next(ch)
                alive.append(ch)
            except StopIteration:
                pass
        chains = alive


def _wkv_kernel(shf_ref, shb_ref, drf_ref, drb_ref, s0_ref, yf_ref, yb_ref, sout_ref, s_ref,
                *, L, cps, nsteps):
    j = pl.program_id(1)

    @pl.when(j == 0)
    def _():
        s_ref[...] = s0_ref[0]

    io = ((shf_ref, drf_ref, yf_ref), (shb_ref, drb_ref, yb_ref))
    order = (list(range(cps)), list(range(cps - 1, -1, -1)))
    results, chains = {}, []
    for d, (sh_ref, dr_ref, _) in enumerate(io):
        m = _wkv_masks(d == 1, L)
        for c in order[d]:
            rows = slice(c * L, (c + 1) * L)
            results[d, c], ch = _wkv_precompute(sh_ref[0, rows, :], dr_ref[0, 0, rows, :], m, L)
            chains += ch
    _round_robin(chains)

    left = lax.broadcasted_iota(jnp.int32, (L, LANES), 1) < HEAD_DIM
    state = [[s_ref[d, p] for p in range(N_PAIRS)] for d in range(2)]
    for i in range(cps):
        old = [[None] * N_PAIRS for _ in range(2)]
        for d in range(2):
            for p in range(N_PAIRS):
                _, _, mc, gg, wtot = results[d, order[d][i]][p]
                s0 = state[d][p]
                old[d][p] = s0
                state[d][p] = s0 * wtot - _mm(s0, mc) + gg
        for d in range(2):
            c = order[d][i]
            for p in range(N_PAIRS):
                q, y0 = results[d, c][p][:2]
                s0 = old[d][p]
                s0_bd = jnp.concatenate([jnp.where(left, s0, 0.0), jnp.where(left, 0.0, s0)], axis=0)
                io[d][2][0, c * L:(c + 1) * L, p * LANES:(p + 1) * LANES] = _mm_nt(q, s0_bd) + y0
    for d in range(2):
        for p in range(N_PAIRS):
            s_ref[d, p] = state[d][p]

    @pl.when(j == nsteps - 1)
    def _():
        sout_ref[0] = s_ref[...]


def _wkv(sh, dr, s0):
    bsz, seq, c3 = sh.shape
    L = CHUNK
    assert L == HEAD_DIM
    cps = CHUNKS_PER_STEP
    tb = cps * L
    assert seq % tb == 0
    ns = seq // tb
    kern = functools.partial(_wkv_kernel, L=L, cps=cps, nsteps=ns)
    st_shape = (1, 2, N_PAIRS, HEAD_DIM, LANES)
    return pl.pallas_call(
        kern,
        grid=(bsz, ns),
        in_specs=[pl.BlockSpec((1, tb, c3), lambda b, j: (b, j, 0)),
                  pl.BlockSpec((1, tb, c3), lambda b, j: (b, ns - 1 - j, 0)),
                  pl.BlockSpec((1, 1, tb, c3), lambda b, j: (0, b, j, 0)),
                  pl.BlockSpec((1, 1, tb, c3), lambda b, j: (1, b, ns - 1 - j, 0)),
                  pl.BlockSpec(st_shape, lambda b, j: (b, 0, 0, 0, 0))],
        out_specs=[pl.BlockSpec((1, tb, D_RWKV), lambda b, j: (b, j, 0)),
                   pl.BlockSpec((1, tb, D_RWKV), lambda b, j: (b, ns - 1 - j, 0)),
                   pl.BlockSpec(st_shape, lambda b, j: (b, 0, 0, 0, 0))],
        out_shape=[jax.ShapeDtypeStruct((bsz, seq, D_RWKV), F32),
                   jax.ShapeDtypeStruct((bsz, seq, D_RWKV), F32),
                   jax.ShapeDtypeStruct((bsz, 2, N_PAIRS, HEAD_DIM, LANES), F32)],
        scratch_shapes=[pltpu.VMEM((2, N_PAIRS, HEAD_DIM, LANES), F32)],
        compiler_params=_params(2),
        name="wkv",
    )(sh, sh, dr, dr, s0)


def _even_out_kernel(x_ref, yf_ref, yb_ref, mg_ref, mod_ref, gn_ref, vec_ref, e_ref, band_ref,
                     icnt_ref, pw_ref, wout_ref, o_ref):
    y = yf_ref[0] + yb_ref[0]
    inv_n = 1.0 / HEAD_DIM
    mean = _split_dot_rhs01(y, e_ref[...]) * inv_n
    yc = y - mean
    var = _split_dot_rhs01(yc * yc, e_ref[...]) * inv_n
    yn = yc * lax.rsqrt(var + GN_EPS) * vec_ref[7:8, :] + vec_ref[8:9, :]
    y_rwkv = (yn + mg_ref[0, :, 0:D_RWKV]) * mg_ref[0, :, D_RWKV:2 * D_RWKV]

    pp = mg_ref[0, :, 2 * D_RWKV:3 * D_RWKV]
    outs = []
    for gi in range(len(POOL_WINDOWS)):
        sl = slice(gi * POOL_GROUP, (gi + 1) * POOL_GROUP)
        xg = pp[:, sl]
        wsum = _split_dot_lhs01(band_ref[gi], xg)
        dg = wsum * icnt_ref[:, sl] - xg
        outs.append(_bdot(dg, pw_ref[gi]))
    y_pool = jnp.concatenate(outs, axis=-1) * vec_ref[9:10, :]

    cat = jnp.concatenate([y_rwkv, y_pool], axis=-1)
    yo = _bdot(cat, wout_ref[...])
    o_ref[0] = x_ref[0] + mod_ref[0, 2:3, :] * _rms(yo, gn_ref[1:2, :])


def _even_out(x, yf, yb, mg, mod, gn, ew, pc):
    bsz, seq, _ = x.shape
    tm = min(TM, seq)
    nt = seq // tm
    c3 = 3 * D_RWKV
    mod_map = (lambda b, j: (b, 0, 0)) if mod.shape[0] == bsz else (lambda b, j: (0, 0, 0))
    tok = lambda w: pl.BlockSpec((1, tm, w), lambda b, j: (b, j, 0))
    return pl.pallas_call(
        _even_out_kernel,
        grid=(bsz, nt),
        in_specs=[tok(D_MODEL), tok(D_RWKV), tok(D_RWKV), tok(c3),
                  pl.BlockSpec((1, 6, D_MODEL), mod_map),
                  _const_spec((4, D_MODEL)),
                  _const_spec((16, D_RWKV)),
                  _const_spec((D_RWKV, D_RWKV)),
                  _const_spec((4, tm, tm)),
                  _const_spec((tm, D_POOL)),
                  _const_spec((4, POOL_GROUP, POOL_GROUP)),
                  _const_spec((D_MODEL, D_MODEL))],
        out_specs=tok(D_MODEL),
        out_shape=jax.ShapeDtypeStruct(x.shape, F32),
        compiler_params=_params(2),
        name="even_out",
    )(x, yf, yb, mg, mod, gn, ew["vec"], ew["seg"], pc["band"], pc["icnt"], ew["pool_w"], ew["w_out"])


def _odd_kernel(x_ref, mod_ref, gn_ref, win_ref, cw_ref, wout_ref, o_ref, *, tm, row_len):
    x = x_ref[0]
    h = _norm_mod(x, gn_ref[0:1, :], mod_ref[0, 0:1, :], mod_ref[0, 1:2, :])
    p = jnp.dot(h.astype(BF16), win_ref[...], preferred_element_type=F32)
    bg = p[:, 0:D_MODEL]
    cg = p[:, D_MODEL:2 * D_MODEL]
    u = p[:, 2 * D_MODEL:3 * D_MODEL]
    first, last = _row_edges(tm, row_len)
    z = bg * _dwconv3(cg * u, cw_ref, first, last, tm)
    yo = _bdot(z, wout_ref[...])
    o_ref[0] = x + mod_ref[0, 2:3, :] * _rms(yo, gn_ref[1:2, :])


def _odd_mix(x, mod, gn, ow, row_len):
    bsz, seq, _ = x.shape
    tm = min(TM, seq)
    nt = seq // tm
    assert tm % row_len == 0 and row_len & (row_len - 1) == 0
    mod_map = (lambda b, j: (b, 0, 0)) if mod.shape[0] == bsz else (lambda b, j: (0, 0, 0))
    tok = pl.BlockSpec((1, tm, D_MODEL), lambda b, j: (b, j, 0))
    return pl.pallas_call(
        functools.partial(_odd_kernel, tm=tm, row_len=row_len),
        grid=(bsz, nt),
        in_specs=[tok, pl.BlockSpec((1, 6, D_MODEL), mod_map),
                  _const_spec((4, D_MODEL)),
                  _const_spec((D_MODEL, 3 * D_MODEL)),
                  _const_spec((3, D_MODEL)),
                  _const_spec((D_MODEL, D_MODEL))],
        out_specs=tok,
        out_shape=jax.ShapeDtypeStruct(x.shape, F32),
        compiler_params=_params(2),
        name="odd_mix",
    )(x, mod, gn, ow["w_in"], ow["conv"], ow["w_out"])


def _ffn_kernel(x_ref, mod_ref, gn_ref, wup_ref, cw_ref, wdn_ref, o_ref, *, tm, row_len):
    x = x_ref[0]
    h = _norm_mod(x, gn_ref[2:3, :], mod_ref[0, 3:4, :], mod_ref[0, 4:5, :])
    p = jnp.dot(h.astype(BF16), wup_ref[...], preferred_element_type=F32)
    first, last = _row_edges(tm, row_len)
    c = _dwconv3(p[:, :D_FF], cw_ref, first, last, tm)
    u = (c * _sigmoid(c)) * p[:, D_FF:]
    yo = _bdot(u, wdn_ref[...])
    o_ref[0] = x + mod_ref[0, 5:6, :] * _rms(yo, gn_ref[3:4, :])


def _ffn(x, mod, gn, fw, row_len):
    bsz, seq, _ = x.shape
    tm = min(TM, seq)
    nt = seq // tm
    assert tm % row_len == 0 and row_len & (row_len - 1) == 0
    mod_map = (lambda b, j: (b, 0, 0)) if mod.shape[0] == bsz else (lambda b, j: (0, 0, 0))
    tok = pl.BlockSpec((1, tm, D_MODEL), lambda b, j: (b, j, 0))
    return pl.pallas_call(
        functools.partial(_ffn_kernel, tm=tm, row_len=row_len),
        grid=(bsz, nt),
        in_specs=[tok, pl.BlockSpec((1, 6, D_MODEL), mod_map),
                  _const_spec((4, D_MODEL)),
                  _const_spec((D_MODEL, 2 * D_FF)),
                  _const_spec((3, D_FF)),
                  _const_spec((D_FF, D_MODEL))],
        out_specs=tok,
        out_shape=jax.ShapeDtypeStruct(x.shape, F32),
        compiler_params=_params(2),
        name="ffn",
    )(x, mod, gn, fw["w_up"], fw["conv"], fw["w_down"])


def _pool_consts(tm, row_len):
    pos = np.arange(tm)
    col = pos % row_len
    same_row = (pos[:, None] // row_len) == (pos[None, :] // row_len)
    band = np.zeros((len(POOL_WINDOWS), tm, tm), np.float32)
    icnt = np.zeros((tm, D_POOL), np.float32)
    for gi, win in enumerate(POOL_WINDOWS):
        lo = np.clip(col - win // 2, 0, row_len)
        hi = np.clip(col + win // 2, 0, row_len)
        inside = (col[None, :] >= lo[:, None]) & (col[None, :] < hi[:, None]) & same_row
        band[gi] = inside.astype(np.float32)
        icnt[:, gi * POOL_GROUP:(gi + 1) * POOL_GROUP] = (1.0 / (hi - lo).astype(np.float32))[:, None]
    return {"band": jnp.asarray(band, BF16), "icnt": jnp.asarray(icnt, F32)}


def _even_weights(i, ev_w_in, ev_w_out, ev_mu, ev_w0, ev_w_up, ev_a0, ev_a_up, ev_g_up, ev_k_k,
                  ev_k_a, ev_r_k, ev_gn_w, ev_gn_b, ev_pool_w, ev_pool_scale):
    o_lw = 3 * D_RWKV
    o_la = o_lw + 2 * D_DECAY_LORA
    o_lg = o_la + 2 * D_AAA_LORA
    o_pool = o_lg + D_GATE_LORA

    def repack(a, with_pool):
        z = lambda n: jnp.zeros(a.shape[:-1] + (n,), a.dtype)
        parts = [a[..., :o_lw], a[..., o_lw:o_la], z(128 - 2 * D_DECAY_LORA),
                 a[..., o_la:o_lg], a[..., o_lg:o_pool], z(128 - D_GATE_LORA)]
        if with_pool:
            parts.append(a[..., o_pool:])
        return jnp.concatenate(parts, axis=-1)

    wup = jnp.zeros((128, 2 * D_RWKV), F32)
    aup = jnp.zeros((128, 2 * D_RWKV), F32)
    for d in range(2):
        wup = wup.at[d * D_DECAY_LORA:(d + 1) * D_DECAY_LORA, d * D_RWKV:(d + 1) * D_RWKV].set(ev_w_up[i, d])
        aup = aup.at[d * D_AAA_LORA:(d + 1) * D_AAA_LORA, d * D_RWKV:(d + 1) * D_RWKV].set(ev_a_up[i, d])
    gup = jnp.zeros((128, D_RWKV), F32).at[:D_GATE_LORA].set(ev_g_up[i])
    vec = jnp.zeros((16, D_RWKV), F32)
    vec = vec.at[0:2].set(ev_w0[i]).at[2:4].set(ev_a0[i]).at[4].set(ev_k_k[i]).at[5].set(ev_k_a[i])
    vec = vec.at[6].set(ev_r_k[i].reshape(D_RWKV)).at[7].set(ev_gn_w[i]).at[8].set(ev_gn_b[i])
    vec = vec.at[9].set(ev_pool_scale[i])
    head = np.arange(D_RWKV) // HEAD_DIM
    seg = jnp.asarray(head[:, None] == head[None, :], BF16)
    return {"w_in": repack(ev_w_in[i], True).astype(BF16), "mu": repack(ev_mu[i], False),
            "wup": wup.astype(BF16), "aup": aup.astype(BF16), "gup": gup.astype(BF16),
            "vec": vec, "seg": seg, "pool_w": ev_pool_w[i].astype(BF16),
            "w_out": ev_w_out[i].astype(BF16)}


def kernel(x, c, ctx, c_ctx, w_mod, b_mod, norm_g, ffn_w_up, ffn_conv, ffn_w_down, ev_w_in, ev_w_out,
           ev_mu, ev_w0, ev_w_up, ev_a0, ev_a_up, ev_g_up, ev_k_k, ev_k_a, ev_r_k, ev_gn_w, ev_gn_b,
           ev_pool_w, ev_pool_scale, od_w_in, od_conv, od_w_out):
    bsz, seq, _ = x.shape
    ctx_len = ctx.shape[1]
    assert seq % TM == 0 and TM % GRID_W == 0 and ctx_len <= TM and ctx_len % CHUNK == 0

    cv = jnp.zeros((16, D_MODEL), F32).at[:bsz].set(c).at[bsz].set(c_ctx)
    mods = _mods(cv, w_mod, b_mod)
    pc_lat = _pool_consts(TM, GRID_W)
    pc_ctx = _pool_consts(ctx_len, ctx_len)

    for layer in range(DEPTH):
        i = layer // 2
        even = layer % 2 == 0
        ctx_later = any(jj % 2 == 0 for jj in range(layer + 1, DEPTH))
        mod = mods[layer, :bsz].reshape(bsz, 6, D_MODEL)
        mod_c = mods[layer, bsz:bsz + 1].reshape(1, 6, D_MODEL)
        gn = norm_g[layer]
        fw = {"w_up": ffn_w_up[layer].astype(BF16), "conv": ffn_conv[layer],
              "w_down": ffn_w_down[layer].astype(BF16)}
        if even:
            ew = _even_weights(i, ev_w_in, ev_w_out, ev_mu, ev_w0, ev_w_up, ev_a0, ev_a_up, ev_g_up,
                               ev_k_k, ev_k_a, ev_r_k, ev_gn_w, ev_gn_b, ev_pool_w, ev_pool_scale)
            sh_c, dr_c, mg_c = _even_in(ctx, mod_c, gn, ew)
            sh_l, dr_l, mg_l = _even_in(x, mod, gn, ew)
            zero = jnp.zeros((bsz, 2, N_PAIRS, HEAD_DIM, LANES), F32)
            yf_c, yb_c, s_c = _wkv(sh_c, dr_c, zero)
            yf_l, yb_l, _ = _wkv(sh_l, dr_l, s_c)
            x_new = _even_out(x, yf_l, yb_l, mg_l, mod, gn, ew, pc_lat)
            if ctx_later:
                ctx = _even_out(ctx, yf_c, yb_c, mg_c, mod_c, gn, ew, pc_ctx)
            x = x_new
        else:
            ow = {"w_in": od_w_in[i].astype(BF16), "conv": od_conv[i], "w_out": od_w_out[i].astype(BF16)}
            x = _odd_mix(x, mod, gn, ow, GRID_W)
            if ctx_later:
                ctx = _odd_mix(ctx, mod_c, gn, ow, ctx_len)
        x = _ffn(x, mod, gn, fw, GRID_W)
        if ctx_later:
            ctx = _ffn(ctx, mod_c, gn, fw, ctx_len)
    return x
```

```python
import functools
import math

import numpy as np
import jax
import jax.numpy as jnp
from jax import lax
from jax.experimental import pallas as pl
from jax.experimental.pallas import tpu as pltpu

D_MODEL = 1024
DEPTH = 4
GRID_W = 64
HEAD_DIM = 64
N_HEADS = 8
D_RWKV = N_HEADS * HEAD_DIM
D_POOL = D_MODEL - D_RWKV
POOL_WINDOWS = (2, 4, 8, 16)
POOL_GROUP = D_POOL // len(POOL_WINDOWS)
D_DECAY_LORA = 32
D_AAA_LORA = 64
D_GATE_LORA = 96
D_FF = 2816
RMS_EPS = 1e-6
GN_EPS = 64e-5

C_LW = 3 * D_RWKV
C_LA = C_LW + 128
C_LG = C_LA + 128
C_SHIFT = C_LG + 128
C_POOL = C_SHIFT
C_EVEN = C_POOL + D_POOL

HALO = 8
CHUNK = 64
CHUNKS_PER_STEP = 4
TM = 512
TM_EVEN_IN = 256
POOL_TILE = 256
FF_CHUNKS = 2

F32 = jnp.float32
BF16 = jnp.bfloat16
HIGHEST = lax.Precision.HIGHEST
VMEM_LIMIT = 48 * 1024 * 1024


def _sigmoid(x):
    return 1.0 / (1.0 + jnp.exp(-x))


DECAY_SCALE = math.exp(-0.5)


def _rms(x, g):
    ms = jnp.mean(x * x, axis=-1, keepdims=True)
    return x * lax.rsqrt(ms + RMS_EPS) * g


def _norm_mod(x, g, shift, scale):
    return _rms(x, g) * (1.0 + scale) + shift


def _bdot(a, b):
    return jnp.dot(a.astype(BF16), b.astype(BF16), preferred_element_type=F32)


def _split_dot_rhs01(x, e):
    hi = x.astype(BF16)
    lo = (x - hi.astype(F32)).astype(BF16)
    return (jnp.dot(hi, e, preferred_element_type=F32)
            + jnp.dot(lo, e, preferred_element_type=F32))


def _split_dot_lhs01(e, x):
    hi = x.astype(BF16)
    lo = (x - hi.astype(F32)).astype(BF16)
    return (jnp.dot(e, hi, preferred_element_type=F32)
            + jnp.dot(e, lo, preferred_element_type=F32))


def _row_edges(tm, row_len):
    pos = lax.broadcasted_iota(jnp.int32, (tm, 1), 0) & (row_len - 1)
    return pos == 0, pos == row_len - 1


def _dwconv3(x, w, first, last, tm):
    prev = jnp.where(first, 0.0, pltpu.roll(x, 1, 0))
    nxt = jnp.where(last, 0.0, pltpu.roll(x, tm - 1, 0))
    return prev * w[0:1, :] + x * w[1:2, :] + nxt * w[2:3, :]


def _const_spec(shape):
    nd = len(shape)
    return pl.BlockSpec(shape, lambda *_: (0,) * nd, pipeline_mode=pl.Buffered(1))


def _params(n_axes):
    return pltpu.CompilerParams(dimension_semantics=("arbitrary",) * n_axes,
                                vmem_limit_bytes=VMEM_LIMIT)


def _mods_kernel(cv_ref, w_ref, b_ref, o_ref):
    cv = cv_ref[...]
    s = cv * _sigmoid(cv)
    o_ref[0] = jnp.dot(s, w_ref[0], preferred_element_type=F32, precision=HIGHEST) + b_ref[0]


def _mods(cv, w_mod, b_mod):
    tn = 1536
    n6 = 6 * D_MODEL
    return pl.pallas_call(
        _mods_kernel,
        grid=(DEPTH, n6 // tn),
        in_specs=[pl.BlockSpec((16, D_MODEL), lambda l, n: (0, 0)),
                  pl.BlockSpec((1, D_MODEL, tn), lambda l, n: (l, 0, n)),
                  pl.BlockSpec((1, 1, tn), lambda l, n: (l, 0, n))],
        out_specs=pl.BlockSpec((1, 16, tn), lambda l, n: (l, 0, n)),
        out_shape=jax.ShapeDtypeStruct((DEPTH, 16, n6), F32),
        compiler_params=_params(2),
        name="mods",
    )(cv, w_mod, b_mod.reshape(DEPTH, 1, n6))


def _even_in_kernel(x_ref, xp_ref, xn_ref, mod_ref, gn_ref, win_ref, mu_ref, wup_ref, aup_ref,
                    gup_ref, vec_ref, e_ref, sh_ref, dr_ref, mg_ref, *, tm, nt):
    j = pl.program_id(1)
    te = tm + 2 * HALO
    xe = jnp.concatenate([xp_ref[0], x_ref[0], xn_ref[0]], axis=0)
    h = _norm_mod(xe, gn_ref[0:1, :], mod_ref[0, 0:1, :], mod_ref[0, 1:2, :])
    p = jnp.dot(h.astype(BF16), win_ref[...], preferred_element_type=F32)
    pc = p[HALO:tm + HALO, :C_SHIFT]
    pe = jnp.concatenate([jnp.where(j == 0, 0.0, p[:HALO, :C_SHIFT]), pc,
                          jnp.where(j == nt - 1, 0.0, p[tm + HALO:, :C_SHIFT])], axis=0)
    prev = pltpu.roll(pe, 1, 0)[HALO:tm + HALO]
    nxt = pltpu.roll(pe, te - 1, 0)[HALO:tm + HALO]
    mu0 = mu_ref[0:1, :]
    mu1 = mu_ref[1:2, :]
    pr = pc * (1.0 - mu0 - mu1) + mu0 * prev + mu1 * nxt

    r = pr[:, 0:D_RWKV]
    k = pr[:, D_RWKV:2 * D_RWKV]
    v = pr[:, 2 * D_RWKV:3 * D_RWKV]
    wd = _bdot(jnp.tanh(pr[:, C_LW:C_LW + 128]), wup_ref[...])
    la = _bdot(pr[:, C_LA:C_LA + 128], aup_ref[...])
    g = _bdot(_sigmoid(pr[:, C_LG:C_LG + 128]), gup_ref[...])

    k_k = vec_ref[4:5, :]
    k_a = vec_ref[5:6, :]
    r_k = vec_ref[6:7, :]
    kkr = k * k_k
    nrm = _split_dot_rhs01(kkr * kkr, e_ref[...])
    kk = kkr * lax.rsqrt(jnp.maximum(nrm, 1e-24))
    kd_sum = None
    for d in range(2):
        wdd = vec_ref[d:d + 1, :] + wd[:, d * D_RWKV:(d + 1) * D_RWKV]
        ld = -DECAY_SCALE * _sigmoid(wdd)
        a = _sigmoid(vec_ref[2 + d:3 + d, :] + la[:, d * D_RWKV:(d + 1) * D_RWKV])
        kd = k * (1.0 + (a - 1.0) * k_a)
        dr_ref[d, 0, :, 0:D_RWKV] = ld
        dr_ref[d, 0, :, D_RWKV:2 * D_RWKV] = kd
        dr_ref[d, 0, :, 2 * D_RWKV:3 * D_RWKV] = kk * a
        kd_sum = kd if kd_sum is None else kd_sum + kd
    coef = _split_dot_rhs01(r * kd_sum * r_k, e_ref[...])
    sh_ref[0, :, 0:D_RWKV] = r
    sh_ref[0, :, D_RWKV:2 * D_RWKV] = v
    sh_ref[0, :, 2 * D_RWKV:3 * D_RWKV] = kk
    mg_ref[0, :, 0:D_RWKV] = coef * v
    mg_ref[0, :, D_RWKV:2 * D_RWKV] = g
    mg_ref[0, :, 2 * D_RWKV:3 * D_RWKV] = p[HALO:tm + HALO, C_POOL:C_EVEN]


def _even_in(x, mod, gn, ew):
    bsz, seq, _ = x.shape
    tm = min(TM_EVEN_IN, seq)
    assert seq % tm == 0
    nt = seq // tm
    hb = tm // HALO
    nhb = seq // HALO
    c3 = 3 * D_RWKV
    kern = functools.partial(_even_in_kernel, tm=tm, nt=nt)
    mod_map = (lambda b, j: (b, 0, 0)) if mod.shape[0] == bsz else (lambda b, j: (0, 0, 0))
    return pl.pallas_call(
        kern,
        grid=(bsz, nt),
        in_specs=[pl.BlockSpec((1, tm, D_MODEL), lambda b, j: (b, j, 0)),
                  pl.BlockSpec((1, HALO, D_MODEL), lambda b, j: (b, jnp.maximum(j * hb - 1, 0), 0)),
                  pl.BlockSpec((1, HALO, D_MODEL), lambda b, j: (b, jnp.minimum((j + 1) * hb, nhb - 1), 0)),
                  pl.BlockSpec((1, 6, D_MODEL), mod_map),
                  _const_spec((4, D_MODEL)),
                  _const_spec((D_MODEL, C_EVEN)),
                  _const_spec((2, C_SHIFT)),
                  _const_spec((128, 2 * D_RWKV)),
                  _const_spec((128, 2 * D_RWKV)),
                  _const_spec((128, D_RWKV)),
                  _const_spec((16, D_RWKV)),
                  _const_spec((D_RWKV, D_RWKV))],
        out_specs=[pl.BlockSpec((1, tm, c3), lambda b, j: (b, j, 0)),
                   pl.BlockSpec((2, 1, tm, c3), lambda b, j: (0, b, j, 0)),
                   pl.BlockSpec((1, tm, c3), lambda b, j: (b, j, 0))],
        out_shape=[jax.ShapeDtypeStruct((bsz, seq, c3), F32),
                   jax.ShapeDtypeStruct((2, bsz, seq, c3), F32),
                   jax.ShapeDtypeStruct((bsz, seq, c3), F32)],
        compiler_params=_params(2),
        name="even_in",
    )(x, x, x, mod, gn, ew["w_in"], ew["mu"], ew["wup"], ew["aup"], ew["gup"], ew["vec"], ew["seg"])


LANES = 128
N_PAIRS = N_HEADS // 2


def _mm(a, b):
    return jnp.dot(a.astype(BF16), b.astype(BF16), preferred_element_type=F32)


def _mm_nt(a, b):
    return lax.dot_general(a.astype(BF16), b.astype(BF16), (((1,), (1,)), ((), ())),
                           preferred_element_type=F32)


def _mm_tn(a, b):
    return lax.dot_general(a.astype(BF16), b.astype(BF16), (((0,), (0,)), ((), ())),
                           preferred_element_type=F32)


def _wkv_masks(reverse, L):
    row = lax.broadcasted_iota(jnp.int32, (L, LANES), 0)
    lane = lax.broadcasted_iota(jnp.int32, (L, LANES), 1)
    colp = lane & (HEAD_DIM - 1)
    row2 = lax.broadcasted_iota(jnp.int32, (2 * L, LANES), 0)
    lane2 = lax.broadcasted_iota(jnp.int32, (2 * L, LANES), 1)
    rowt = lax.broadcasted_iota(jnp.int32, (L, L), 0)
    colt = lax.broadcasted_iota(jnp.int32, (L, L), 1)
    blk = row ^ colp
    m = {
        "left": lane < HEAD_DIM,
        "incl": (row <= colp) if reverse else (row >= colp),
        "strict": (row < colp) if reverse else (row > colp),
        "eye": (row == colp).astype(F32),
        "diag_blocks": (row2 < L) == (lane2 < HEAD_DIM),
        "tri": ((rowt <= colt) if reverse else (rowt >= colt)).astype(F32),
        "same2": (blk >> 1) == 0,
    }
    g, lg = 2, 1
    while g < L:
        m["off%d" % g] = (blk >> lg) == 1
        g, lg = 2 * g, lg + 1
    return m


def _wkv_precompute(sh, dr, m, L):
    left = m["left"]

    def bd(x):
        return jnp.concatenate([jnp.where(left, x, 0.0), jnp.where(left, 0.0, x)], axis=0)

    def pd(y):
        return jnp.where(left, y[:L], y[L:])

    r = sh[:, 0:D_RWKV]
    v = sh[:, D_RWKV:2 * D_RWKV]
    kk = sh[:, 2 * D_RWKV:3 * D_RWKV]
    ld = dr[:, 0:D_RWKV]
    kd = dr[:, D_RWKV:2 * D_RWKV]
    b = dr[:, 2 * D_RWKV:3 * D_RWKV]

    cum = jnp.dot(m["tri"], ld, preferred_element_type=F32, precision=HIGHEST)
    tot = jnp.sum(ld, axis=0, keepdims=True)
    kkt = kk * jnp.exp(cum - ld)
    rt = r * jnp.exp(cum)
    iw = jnp.exp(-cum)
    kh = kd * iw
    bh = b * iw
    wl = jnp.exp(tot - cum)
    khw = kd * wl
    bhw = b * wl
    wtot = jnp.exp(tot)

    def pair_chain(p, res):
        sl = slice(p * LANES, (p + 1) * LANES)
        kkt_p, rt_p, kh_p, bh_p = kkt[:, sl], rt[:, sl], kh[:, sl], bh[:, sl]
        v_p, khw_p, bhw_p = v[:, sl], khw[:, sl], bhw[:, sl]
        lhs = jnp.concatenate([kkt_p, rt_p], axis=0)
        sb = _mm_nt(lhs, bd(bh_p))
        sk = _mm_nt(lhs, bd(kh_p))
        yield
        a_ab = jnp.where(m["strict"], sb[:L], 0.0)
        a_rb = jnp.where(m["incl"], sb[L:], 0.0)
        a_ak = jnp.where(m["strict"], sk[:L], 0.0)
        a_rk = jnp.where(m["incl"], sk[L:], 0.0)
        avk = _mm(jnp.concatenate([a_ak, a_rk], axis=0), bd(v_p))
        t = m["eye"] - jnp.where(m["same2"], a_ab, 0.0)
        g = 2
        while g < L:
            w = _mm(t, bd(jnp.where(m["off%d" % g], a_ab, 0.0)))
            yield
            t = t - _mm(w, bd(t))
            yield
            g *= 2
        pu = _mm(t, jnp.concatenate([bd(kkt_p), bd(avk[:L])], axis=1))
        yield
        pm = pu[:, :LANES]
        u0 = pu[:, LANES:]
        qy = _mm(a_rb, jnp.concatenate([bd(pm), bd(u0)], axis=1))
        mc = jnp.where(m["diag_blocks"], _mm_tn(pm, bhw_p), 0.0)
        gg = pd(_mm_tn(jnp.concatenate([v_p, u0], axis=0),
                       jnp.concatenate([khw_p, -bhw_p], axis=0)))
        q = rt_p - qy[:, :LANES]
        y0 = avk[L:] - qy[:, LANES:]
        res[p] = (q, y0, mc, gg, wtot[:, sl])

    res = [None] * N_PAIRS
    return res, [pair_chain(p, res) for p in range(N_PAIRS)]


def _round_robin(chains):
    chains = list(chains)
    while chains:
        alive = []
        for ch in chains:
            try:
                next(ch)
                alive.append(ch)
            except StopIteration:
                pass
        chains = alive


def _wkv_kernel(shf_ref, shb_ref, drf_ref, drb_ref, s0_ref, yf_ref, yb_ref, sout_ref, s_ref,
                *, L, cps, nsteps):
    j = pl.program_id(1)

    @pl.when(j == 0)
    def _():
        s_ref[...] = s0_ref[0]

    io = ((shf_ref, drf_ref, yf_ref), (shb_ref, drb_ref, yb_ref))
    order = (list(range(cps)), list(range(cps - 1, -1, -1)))
    results, chains = {}, []
    for d, (sh_ref, dr_ref, _) in enumerate(io):
        m = _wkv_masks(d == 1, L)
        for c in order[d]:
            rows = slice(c * L, (c + 1) * L)
            results[d, c], ch = _wkv_precompute(sh_ref[0, rows, :], dr_ref[0, 0, rows, :], m, L)
            chains += ch
    _round_robin(chains)

    left = lax.broadcasted_iota(jnp.int32, (L, LANES), 1) < HEAD_DIM
    state = [[s_ref[d, p] for p in range(N_PAIRS)] for d in range(2)]
    for i in range(cps):
        old = [[None] * N_PAIRS for _ in range(2)]
        for d in range(2):
            for p in range(N_PAIRS):
                _, _, mc, gg, wtot = results[d, order[d][i]][p]
                s0 = state[d][p]
                old[d][p] = s0
                state[d][p] = s0 * wtot - _mm(s0, mc) + gg
        for d in range(2):
            c = order[d][i]
            for p in range(N_PAIRS):
                q, y0 = results[d, c][p][:2]
                s0 = old[d][p]
                s0_bd = jnp.concatenate([jnp.where(left, s0, 0.0), jnp.where(left, 0.0, s0)], axis=0)
                io[d][2][0, c * L:(c + 1) * L, p * LANES:(p + 1) * LANES] = _mm_nt(q, s0_bd) + y0
    for d in range(2):
        for p in range(N_PAIRS):
            s_ref[d, p] = state[d][p]

    @pl.when(j == nsteps - 1)
    def _():
        sout_ref[0] = s_ref[...]


def _wkv(sh, dr, s0):
    bsz, seq, c3 = sh.shape
    L = CHUNK
    assert L == HEAD_DIM
    cps = CHUNKS_PER_STEP
    tb = cps * L
    assert seq % tb == 0
    ns = seq // tb
    kern = functools.partial(_wkv_kernel, L=L, cps=cps, nsteps=ns)
    st_shape = (1, 2, N_PAIRS, HEAD_DIM, LANES)
    return pl.pallas_call(
        kern,
        grid=(bsz, ns),
        in_specs=[pl.BlockSpec((1, tb, c3), lambda b, j: (b, j, 0)),
                  pl.BlockSpec((1, tb, c3), lambda b, j: (b, ns - 1 - j, 0)),
                  pl.BlockSpec((1, 1, tb, c3), lambda b, j: (0, b, j, 0)),
                  pl.BlockSpec((1, 1, tb, c3), lambda b, j: (1, b, ns - 1 - j, 0)),
                  pl.BlockSpec(st_shape, lambda b, j: (b, 0, 0, 0, 0))],
        out_specs=[pl.BlockSpec((1, tb, D_RWKV), lambda b, j: (b, j, 0)),
                   pl.BlockSpec((1, tb, D_RWKV), lambda b, j: (b, ns - 1 - j, 0)),
                   pl.BlockSpec(st_shape, lambda b, j: (b, 0, 0, 0, 0))],
        out_shape=[jax.ShapeDtypeStruct((bsz, seq, D_RWKV), F32),
                   jax.ShapeDtypeStruct((bsz, seq, D_RWKV), F32),
                   jax.ShapeDtypeStruct((bsz, 2, N_PAIRS, HEAD_DIM, LANES), F32)],
        scratch_shapes=[pltpu.VMEM((2, N_PAIRS, HEAD_DIM, LANES), F32)],
        compiler_params=_params(2),
        name="wkv",
    )(sh, sh, dr, dr, s0)


def _even_out_kernel(x_ref, yf_ref, yb_ref, mg_ref, mod_ref, gn_ref, vec_ref, e_ref, band_ref,
                     icnt_ref, pw_ref, wout_ref, o_ref, *, tm, pt):
    y = yf_ref[0] + yb_ref[0]
    inv_n = 1.0 / HEAD_DIM
    mean = _split_dot_rhs01(y, e_ref[...]) * inv_n
    yc = y - mean
    var = _split_dot_rhs01(yc * yc, e_ref[...]) * inv_n
    yn = yc * lax.rsqrt(var + GN_EPS) * vec_ref[7:8, :] + vec_ref[8:9, :]
    y_rwkv = (yn + mg_ref[0, :, 0:D_RWKV]) * mg_ref[0, :, D_RWKV:2 * D_RWKV]

    pp = mg_ref[0, :, 2 * D_RWKV:3 * D_RWKV]
    outs = []
    for gi in range(len(POOL_WINDOWS)):
        sl = slice(gi * POOL_GROUP, (gi + 1) * POOL_GROUP)
        xg = pp[:, sl]
        wsum = jnp.concatenate([_split_dot_lhs01(band_ref[gi], xg[s * pt:(s + 1) * pt])
                                for s in range(tm // pt)], axis=0)
        icnt = jnp.concatenate([icnt_ref[:, sl]] * (tm // pt), axis=0)
        dg = wsum * icnt - xg
        outs.append(_bdot(dg, pw_ref[gi]))
    y_pool = jnp.concatenate(outs, axis=-1) * vec_ref[9:10, :]

    cat = jnp.concatenate([y_rwkv, y_pool], axis=-1)
    yo = _bdot(cat, wout_ref[...])
    o_ref[0] = x_ref[0] + mod_ref[0, 2:3, :] * _rms(yo, gn_ref[1:2, :])


def _even_out(x, yf, yb, mg, mod, gn, ew, pc):
    bsz, seq, _ = x.shape
    tm = min(TM, seq)
    nt = seq // tm
    c3 = 3 * D_RWKV
    mod_map = (lambda b, j: (b, 0, 0)) if mod.shape[0] == bsz else (lambda b, j: (0, 0, 0))
    tok = lambda w: pl.BlockSpec((1, tm, w), lambda b, j: (b, j, 0))
    pt = pc["band"].shape[1]
    assert tm % pt == 0
    return pl.pallas_call(
        functools.partial(_even_out_kernel, tm=tm, pt=pt),
        grid=(bsz, nt),
        in_specs=[tok(D_MODEL), tok(D_RWKV), tok(D_RWKV), tok(c3),
                  pl.BlockSpec((1, 6, D_MODEL), mod_map),
                  _const_spec((4, D_MODEL)),
                  _const_spec((16, D_RWKV)),
                  _const_spec((D_RWKV, D_RWKV)),
                  _const_spec((4, pt, pt)),
                  _const_spec((pt, D_POOL)),
                  _const_spec((4, POOL_GROUP, POOL_GROUP)),
                  _const_spec((D_MODEL, D_MODEL))],
        out_specs=tok(D_MODEL),
        out_shape=jax.ShapeDtypeStruct(x.shape, F32),
        compiler_params=_params(2),
        name="even_out",
    )(x, yf, yb, mg, mod, gn, ew["vec"], ew["seg"], pc["band"], pc["icnt"], ew["pool_w"], ew["w_out"])


def _odd_kernel(x_ref, mod_ref, gn_ref, win_ref, cw_ref, wout_ref, o_ref, *, tm, row_len):
    x = x_ref[0]
    h = _norm_mod(x, gn_ref[0:1, :], mod_ref[0, 0:1, :], mod_ref[0, 1:2, :])
    p = jnp.dot(h.astype(BF16), win_ref[...], preferred_element_type=F32)
    bg = p[:, 0:D_MODEL]
    cg = p[:, D_MODEL:2 * D_MODEL]
    u = p[:, 2 * D_MODEL:3 * D_MODEL]
    first, last = _row_edges(tm, row_len)
    z = bg * _dwconv3(cg * u, cw_ref[...], first, last, tm)
    yo = _bdot(z, wout_ref[...])
    o_ref[0] = x + mod_ref[0, 2:3, :] * _rms(yo, gn_ref[1:2, :])


def _odd_mix(x, mod, gn, ow, row_len):
    bsz, seq, _ = x.shape
    tm = min(TM, seq)
    nt = seq // tm
    assert tm % row_len == 0 and row_len & (row_len - 1) == 0
    mod_map = (lambda b, j: (b, 0, 0)) if mod.shape[0] == bsz else (lambda b, j: (0, 0, 0))
    tok = pl.BlockSpec((1, tm, D_MODEL), lambda b, j: (b, j, 0))
    return pl.pallas_call(
        functools.partial(_odd_kernel, tm=tm, row_len=row_len),
        grid=(bsz, nt),
        in_specs=[tok, pl.BlockSpec((1, 6, D_MODEL), mod_map),
                  _const_spec((4, D_MODEL)),
                  _const_spec((D_MODEL, 3 * D_MODEL)),
                  _const_spec((3, D_MODEL)),
                  _const_spec((D_MODEL, D_MODEL))],
        out_specs=tok,
        out_shape=jax.ShapeDtypeStruct(x.shape, F32),
        compiler_params=_params(2),
        name="odd_mix",
    )(x, mod, gn, ow["w_in"], ow["conv"], ow["w_out"])


def _ffn_kernel(x_ref, mod_ref, gn_ref, wup_ref, cw_ref, wdn_ref, o_ref, *, tm, row_len):
    x = x_ref[0]
    h = _norm_mod(x, gn_ref[2:3, :], mod_ref[0, 3:4, :], mod_ref[0, 4:5, :]).astype(BF16)
    first, last = _row_edges(tm, row_len)
    fc = D_FF // FF_CHUNKS
    yo = None
    for i in range(FF_CHUNKS):
        a = jnp.dot(h, wup_ref[:, i * fc:(i + 1) * fc], preferred_element_type=F32)
        gate = jnp.dot(h, wup_ref[:, D_FF + i * fc:D_FF + (i + 1) * fc], preferred_element_type=F32)
        c = _dwconv3(a, cw_ref[:, i * fc:(i + 1) * fc], first, last, tm)
        u = (c * _sigmoid(c)) * gate
        part = jnp.dot(u.astype(BF16), wdn_ref[i * fc:(i + 1) * fc, :], preferred_element_type=F32)
        yo = part if yo is None else yo + part
    o_ref[0] = x + mod_ref[0, 5:6, :] * _rms(yo, gn_ref[3:4, :])


def _ffn(x, mod, gn, fw, row_len):
    bsz, seq, _ = x.shape
    tm = min(TM, seq)
    nt = seq // tm
    assert tm % row_len == 0 and row_len & (row_len - 1) == 0 and D_FF % (FF_CHUNKS * LANES) == 0
    mod_map = (lambda b, j: (b, 0, 0)) if mod.shape[0] == bsz else (lambda b, j: (0, 0, 0))
    tok = pl.BlockSpec((1, tm, D_MODEL), lambda b, j: (b, j, 0))
    return pl.pallas_call(
        functools.partial(_ffn_kernel, tm=tm, row_len=row_len),
        grid=(bsz, nt),
        in_specs=[tok, pl.BlockSpec((1, 6, D_MODEL), mod_map),
                  _const_spec((4, D_MODEL)),
                  _const_spec((D_MODEL, 2 * D_FF)),
                  _const_spec((3, D_FF)),
                  _const_spec((D_FF, D_MODEL))],
        out_specs=tok,
        out_shape=jax.ShapeDtypeStruct(x.shape, F32),
        compiler_params=_params(2),
        name="ffn",
    )(x, mod, gn, fw["w_up"], fw["conv"], fw["w_down"])


def _pool_consts(tm, row_len):
    pos = np.arange(tm)
    col = pos % row_len
    same_row = (pos[:, None] // row_len) == (pos[None, :] // row_len)
    band = np.zeros((len(POOL_WINDOWS), tm, tm), np.float32)
    icnt = np.zeros((tm, D_POOL), np.float32)
    for gi, win in enumerate(POOL_WINDOWS):
        lo = np.clip(col - win // 2, 0, row_len)
        hi = np.clip(col + win // 2, 0, row_len)
        inside = (col[None, :] >= lo[:, None]) & (col[None, :] < hi[:, None]) & same_row
        band[gi] = inside.astype(np.float32)
        icnt[:, gi * POOL_GROUP:(gi + 1) * POOL_GROUP] = (1.0 / (hi - lo).astype(np.float32))[:, None]
    return {"band": jnp.asarray(band, BF16), "icnt": jnp.asarray(icnt, F32)}


def _even_weights(i, ev_w_in, ev_w_out, ev_mu, ev_w0, ev_w_up, ev_a0, ev_a_up, ev_g_up, ev_k_k,
                  ev_k_a, ev_r_k, ev_gn_w, ev_gn_b, ev_pool_w, ev_pool_scale):
    o_lw = 3 * D_RWKV
    o_la = o_lw + 2 * D_DECAY_LORA
    o_lg = o_la + 2 * D_AAA_LORA
    o_pool = o_lg + D_GATE_LORA

    def repack(a, with_pool):
        z = lambda n: jnp.zeros(a.shape[:-1] + (n,), a.dtype)
        parts = [a[..., :o_lw], a[..., o_lw:o_la], z(128 - 2 * D_DECAY_LORA),
                 a[..., o_la:o_lg], a[..., o_lg:o_pool], z(128 - D_GATE_LORA)]
        if with_pool:
            parts.append(a[..., o_pool:])
        return jnp.concatenate(parts, axis=-1)

    def per_dir(w):
        rows = jnp.concatenate([jnp.pad(w[0], ((0, 0), (0, D_RWKV))), jnp.pad(w[1], ((0, 0), (D_RWKV, 0)))], axis=0)
        return jnp.pad(rows, ((0, 128 - rows.shape[0]), (0, 0)))

    wup = per_dir(ev_w_up[i])
    aup = per_dir(ev_a_up[i])
    gup = jnp.pad(ev_g_up[i], ((0, 128 - D_GATE_LORA), (0, 0)))
    row = lambda a: a.reshape(1, D_RWKV)
    vec = jnp.concatenate([ev_w0[i], ev_a0[i], row(ev_k_k[i]), row(ev_k_a[i]), row(ev_r_k[i]), row(ev_gn_w[i]),
                           row(ev_gn_b[i]), row(ev_pool_scale[i]), jnp.zeros((6, D_RWKV), F32)], axis=0)
    head = np.arange(D_RWKV) // HEAD_DIM
    seg = jnp.asarray(head[:, None] == head[None, :], BF16)
    return {"w_in": repack(ev_w_in[i], True).astype(BF16), "mu": repack(ev_mu[i], False),
            "wup": wup.astype(BF16), "aup": aup.astype(BF16), "gup": gup.astype(BF16),
            "vec": vec, "seg": seg, "pool_w": ev_pool_w[i].astype(BF16),
            "w_out": ev_w_out[i].astype(BF16)}


def kernel(x, c, ctx, c_ctx, w_mod, b_mod, norm_g, ffn_w_up, ffn_conv, ffn_w_down, ev_w_in, ev_w_out,
           ev_mu, ev_w0, ev_w_up, ev_a0, ev_a_up, ev_g_up, ev_k_k, ev_k_a, ev_r_k, ev_gn_w, ev_gn_b,
           ev_pool_w, ev_pool_scale, od_w_in, od_conv, od_w_out):
    bsz, seq, _ = x.shape
    ctx_len = ctx.shape[1]
    assert seq % TM == 0 and TM % POOL_TILE == 0 and POOL_TILE % GRID_W == 0
    assert ctx_len <= TM and ctx_len % (CHUNK * CHUNKS_PER_STEP) == 0 and bsz < 16

    cv = jnp.concatenate([c, c_ctx[None], jnp.zeros((15 - bsz, D_MODEL), F32)], axis=0)
    mods = _mods(cv, w_mod, b_mod)
    pc_lat = _pool_consts(POOL_TILE, GRID_W)
    pc_ctx = _pool_consts(ctx_len, ctx_len)
    ffn_w_up_b = ffn_w_up.astype(BF16)
    ffn_w_down_b = ffn_w_down.astype(BF16)
    od_w_in_b = od_w_in.astype(BF16)
    od_w_out_b = od_w_out.astype(BF16)

    for layer in range(DEPTH):
        i = layer // 2
        even = layer % 2 == 0
        ctx_later = any(jj % 2 == 0 for jj in range(layer + 1, DEPTH))
        mod = mods[layer, :bsz].reshape(bsz, 6, D_MODEL)
        mod_c = mods[layer, bsz:bsz + 1].reshape(1, 6, D_MODEL)
        gn = norm_g[layer]
        fw = {"w_up": ffn_w_up_b[layer], "conv": ffn_conv[layer], "w_down": ffn_w_down_b[layer]}
        if even:
            ew = _even_weights(i, ev_w_in, ev_w_out, ev_mu, ev_w0, ev_w_up, ev_a0, ev_a_up, ev_g_up,
                               ev_k_k, ev_k_a, ev_r_k, ev_gn_w, ev_gn_b, ev_pool_w, ev_pool_scale)
            sh_c, dr_c, mg_c = _even_in(ctx, mod_c, gn, ew)
            sh_l, dr_l, mg_l = _even_in(x, mod, gn, ew)
            zero = jnp.zeros((bsz, 2, N_PAIRS, HEAD_DIM, LANES), F32)
            yf_c, yb_c, s_c = _wkv(sh_c, dr_c, zero)
            yf_l, yb_l, _ = _wkv(sh_l, dr_l, s_c)
            x_new = _even_out(x, yf_l, yb_l, mg_l, mod, gn, ew, pc_lat)
            if ctx_later:
                ctx = _even_out(ctx, yf_c, yb_c, mg_c, mod_c, gn, ew, pc_ctx)
            x = x_new
        else:
            ow = {"w_in": od_w_in_b[i], "conv": od_conv[i], "w_out": od_w_out_b[i]}
            x = _odd_mix(x, mod, gn, ow, GRID_W)
            if ctx_later:
                ctx = _odd_mix(ctx, mod_c, gn, ow, ctx_len)
        x = _ffn(x, mod, gn, fw, GRID_W)
        if ctx_later:
            ctx = _ffn(ctx, mod_c, gn, fw, ctx_len)
    return x
```

```python
import functools
import math

import numpy as np
import jax
import jax.numpy as jnp
from jax import lax
from jax.experimental import pallas as pl
from jax.experimental.pallas import tpu as pltpu

D_MODEL = 1024
DEPTH = 4
GRID_W = 64
HEAD_DIM = 64
N_HEADS = 8
D_RWKV = N_HEADS * HEAD_DIM
D_POOL = D_MODEL - D_RWKV
POOL_WINDOWS = (2, 4, 8, 16)
POOL_GROUP = D_POOL // len(POOL_WINDOWS)
D_DECAY_LORA = 32
D_AAA_LORA = 64
D_GATE_LORA = 96
D_FF = 2816
RMS_EPS = 1e-6
GN_EPS = 64e-5
DECAY_SCALE = math.exp(-0.5)

LANES = 128
SUBLANES = 8
N_PAIRS = N_HEADS // 2

C_LW = 3 * D_RWKV
C_LA = C_LW + LANES
C_LG = C_LA + LANES
C_SHIFT = C_LG + LANES
C_POOL = C_SHIFT
C_EVEN = C_POOL + D_POOL

HALO = SUBLANES
CHUNK = 64
CHUNKS_PER_STEP = 4
TM_EVEN_IN = 256
TM_MIX = 256
POOL_TILE = 256
FF_CHUNKS = 1

F32 = jnp.float32
BF16 = jnp.bfloat16
HIGHEST = lax.Precision.HIGHEST
VMEM_LIMIT = 56 * 1024 * 1024


def _sigmoid(x):
    return 1.0 / (1.0 + jnp.exp(-x))


def _rms(x, g):
    ms = jnp.mean(x * x, axis=-1, keepdims=True)
    return x * lax.rsqrt(ms + RMS_EPS) * g


def _norm_mod(x, g, shift, scale):
    return _rms(x, g) * (1.0 + scale) + shift


def _bdot(a, b):
    return jnp.dot(a.astype(BF16), b.astype(BF16), preferred_element_type=F32)


def _split_dot_rhs01(x, e):
    hi = x.astype(BF16)
    lo = (x - hi.astype(F32)).astype(BF16)
    return (jnp.dot(hi, e, preferred_element_type=F32)
            + jnp.dot(lo, e, preferred_element_type=F32))


def _split_dot_lhs01(e, x):
    hi = x.astype(BF16)
    lo = (x - hi.astype(F32)).astype(BF16)
    return (jnp.dot(e, hi, preferred_element_type=F32)
            + jnp.dot(e, lo, preferred_element_type=F32))


def _row_edges(tm, row_len):
    pos = lax.broadcasted_iota(jnp.int32, (tm, 1), 0) & (row_len - 1)
    return pos == 0, pos == row_len - 1


def _dwconv3(x, w, first, last, tm):
    prev = jnp.where(first, 0.0, pltpu.roll(x, 1, 0))
    nxt = jnp.where(last, 0.0, pltpu.roll(x, tm - 1, 0))
    return prev * w[0:1, :] + x * w[1:2, :] + nxt * w[2:3, :]


def _const_spec(shape):
    nd = len(shape)
    return pl.BlockSpec(shape, lambda *_: (0,) * nd, pipeline_mode=pl.Buffered(1))


def _params(n_axes):
    return pltpu.CompilerParams(dimension_semantics=("arbitrary",) * n_axes,
                                vmem_limit_bytes=VMEM_LIMIT)


def _mods_kernel(cv_ref, w_ref, b_ref, o_ref):
    cv = cv_ref[...]
    s = cv * _sigmoid(cv)
    o_ref[0] = jnp.dot(s, w_ref[0], preferred_element_type=F32, precision=HIGHEST) + b_ref[0]


def _mods(cv, w_mod, b_mod):
    tn = 1536
    n6 = 6 * D_MODEL
    return pl.pallas_call(
        _mods_kernel,
        grid=(DEPTH, n6 // tn),
        in_specs=[pl.BlockSpec((16, D_MODEL), lambda l, n: (0, 0)),
                  pl.BlockSpec((1, D_MODEL, tn), lambda l, n: (l, 0, n)),
                  pl.BlockSpec((1, 1, tn), lambda l, n: (l, 0, n))],
        out_specs=pl.BlockSpec((1, 16, tn), lambda l, n: (l, 0, n)),
        out_shape=jax.ShapeDtypeStruct((DEPTH, 16, n6), F32),
        compiler_params=_params(2),
        name="mods",
    )(cv, w_mod, b_mod.reshape(DEPTH, 1, n6))


def _even_in_kernel(x_ref, xp_ref, xn_ref, mod_ref, gn_ref, win_ref, mu_ref, wup_ref, aup_ref,
                    gup_ref, vec_ref, e_ref, sh_ref, dr_ref, mg_ref, *, tm, nt):
    j = pl.program_id(1)
    te = tm + 2 * HALO
    xe = jnp.concatenate([xp_ref[0], x_ref[0], xn_ref[0]], axis=0)
    h = _norm_mod(xe, gn_ref[0:1, :], mod_ref[0, 0:1, :], mod_ref[0, 1:2, :])
    p = jnp.dot(h.astype(BF16), win_ref[...], preferred_element_type=F32)
    pc = p[HALO:tm + HALO, :C_SHIFT]
    pe = jnp.concatenate([jnp.where(j == 0, 0.0, p[:HALO, :C_SHIFT]), pc,
                          jnp.where(j == nt - 1, 0.0, p[tm + HALO:, :C_SHIFT])], axis=0)
    prev = pltpu.roll(pe, 1, 0)[HALO:tm + HALO]
    nxt = pltpu.roll(pe, te - 1, 0)[HALO:tm + HALO]
    mu0 = mu_ref[0:1, :]
    mu1 = mu_ref[1:2, :]
    pr = pc * (1.0 - mu0 - mu1) + mu0 * prev + mu1 * nxt

    r = pr[:, 0:D_RWKV]
    k = pr[:, D_RWKV:2 * D_RWKV]
    v = pr[:, 2 * D_RWKV:3 * D_RWKV]
    wd = _bdot(jnp.tanh(pr[:, C_LW:C_LW + LANES]), wup_ref[...])
    la = _bdot(pr[:, C_LA:C_LA + LANES], aup_ref[...])
    g = _bdot(_sigmoid(pr[:, C_LG:C_LG + LANES]), gup_ref[...])

    k_k = vec_ref[4:5, :]
    k_a = vec_ref[5:6, :]
    r_k = vec_ref[6:7, :]
    kkr = k * k_k
    nrm = _split_dot_rhs01(kkr * kkr, e_ref[...])
    kk = kkr * lax.rsqrt(jnp.maximum(nrm, 1e-24))
    kd_sum = None
    for d in range(2):
        wdd = vec_ref[d:d + 1, :] + wd[:, d * D_RWKV:(d + 1) * D_RWKV]
        ld = -DECAY_SCALE * _sigmoid(wdd)
        a = _sigmoid(vec_ref[2 + d:3 + d, :] + la[:, d * D_RWKV:(d + 1) * D_RWKV])
        kd = k * (1.0 + (a - 1.0) * k_a)
        dr_ref[d, 0, :, 0:D_RWKV] = ld
        dr_ref[d, 0, :, D_RWKV:2 * D_RWKV] = kd
        dr_ref[d, 0, :, 2 * D_RWKV:3 * D_RWKV] = kk * a
        kd_sum = kd if kd_sum is None else kd_sum + kd
    coef = _split_dot_rhs01(r * kd_sum * r_k, e_ref[...])
    sh_ref[0, :, 0:D_RWKV] = r
    sh_ref[0, :, D_RWKV:2 * D_RWKV] = v
    sh_ref[0, :, 2 * D_RWKV:3 * D_RWKV] = kk
    mg_ref[0, :, 0:D_RWKV] = coef * v
    mg_ref[0, :, D_RWKV:2 * D_RWKV] = g
    mg_ref[0, :, 2 * D_RWKV:3 * D_RWKV] = p[HALO:tm + HALO, C_POOL:C_EVEN]


def _even_in(x, mod, gn, ew):
    bsz, seq, _ = x.shape
    tm = min(TM_EVEN_IN, seq)
    assert seq % tm == 0
    nt = seq // tm
    hb = tm // HALO
    nhb = seq // HALO
    c3 = 3 * D_RWKV
    kern = functools.partial(_even_in_kernel, tm=tm, nt=nt)
    mod_map = (lambda b, j: (b, 0, 0)) if mod.shape[0] == bsz else (lambda b, j: (0, 0, 0))
    return pl.pallas_call(
        kern,
        grid=(bsz, nt),
        in_specs=[pl.BlockSpec((1, tm, D_MODEL), lambda b, j: (b, j, 0)),
                  pl.BlockSpec((1, HALO, D_MODEL), lambda b, j: (b, jnp.maximum(j * hb - 1, 0), 0)),
                  pl.BlockSpec((1, HALO, D_MODEL), lambda b, j: (b, jnp.minimum((j + 1) * hb, nhb - 1), 0)),
                  pl.BlockSpec((1, 6, D_MODEL), mod_map),
                  _const_spec((4, D_MODEL)),
                  _const_spec((D_MODEL, C_EVEN)),
                  _const_spec((2, C_SHIFT)),
                  _const_spec((LANES, 2 * D_RWKV)),
                  _const_spec((LANES, 2 * D_RWKV)),
                  _const_spec((LANES, D_RWKV)),
                  _const_spec((16, D_RWKV)),
                  _const_spec((D_RWKV, D_RWKV))],
        out_specs=[pl.BlockSpec((1, tm, c3), lambda b, j: (b, j, 0)),
                   pl.BlockSpec((2, 1, tm, c3), lambda b, j: (0, b, j, 0)),
                   pl.BlockSpec((1, tm, c3), lambda b, j: (b, j, 0))],
        out_shape=[jax.ShapeDtypeStruct((bsz, seq, c3), F32),
                   jax.ShapeDtypeStruct((2, bsz, seq, c3), F32),
                   jax.ShapeDtypeStruct((bsz, seq, c3), F32)],
        compiler_params=_params(2),
        name="even_in",
    )(x, x, x, mod, gn, ew["w_in"], ew["mu"], ew["wup"], ew["aup"], ew["gup"], ew["vec"], ew["seg"])


def _mm(a, b):
    return jnp.dot(a.astype(BF16), b.astype(BF16), preferred_element_type=F32)


def _mm_nt(a, b):
    return lax.dot_general(a.astype(BF16), b.astype(BF16), (((1,), (1,)), ((), ())),
                           preferred_element_type=F32)


def _mm_tn(a, b):
    return lax.dot_general(a.astype(BF16), b.astype(BF16), (((0,), (0,)), ((), ())),
                           preferred_element_type=F32)


def _cumsum_rows(x, reverse, n):
    rows = lax.broadcasted_iota(jnp.int32, (n, 1), 0)
    s = 1
    while s < n:
        if s < SUBLANES:
            if reverse:
                shifted = jnp.where(rows < n - s, pltpu.roll(x, n - s, 0), 0.0)
            else:
                shifted = jnp.where(rows >= s, pltpu.roll(x, s, 0), 0.0)
        else:
            zeros = jnp.zeros((s, x.shape[1]), x.dtype)
            shifted = (jnp.concatenate([x[s:], zeros], axis=0) if reverse
                       else jnp.concatenate([zeros, x[:n - s]], axis=0))
        x = x + shifted
        s *= 2
    return x


def _wkv_masks(reverse, L):
    row = lax.broadcasted_iota(jnp.int32, (L, LANES), 0)
    lane = lax.broadcasted_iota(jnp.int32, (L, LANES), 1)
    colp = lane & (HEAD_DIM - 1)
    row2 = lax.broadcasted_iota(jnp.int32, (2 * L, LANES), 0)
    lane2 = lax.broadcasted_iota(jnp.int32, (2 * L, LANES), 1)
    blk = row ^ colp
    m = {
        "reverse": reverse,
        "left": lane < HEAD_DIM,
        "incl": (row <= colp) if reverse else (row >= colp),
        "strict": (row < colp) if reverse else (row > colp),
        "eye": (row == colp).astype(F32),
        "diag_blocks": (row2 < L) == (lane2 < HEAD_DIM),
        "same2": (blk >> 1) == 0,
    }
    g, lg = 2, 1
    while g < L:
        m["off%d" % g] = (blk >> lg) == 1
        g, lg = 2 * g, lg + 1
    return m


def _wkv_precompute(sh, dr, m, L):
    left = m["left"]

    def bd(x):
        return jnp.concatenate([jnp.where(left, x, 0.0), jnp.where(left, 0.0, x)], axis=0)

    def pd(y):
        return jnp.where(left, y[:L], y[L:])

    r = sh[:, 0:D_RWKV]
    v = sh[:, D_RWKV:2 * D_RWKV]
    kk = sh[:, 2 * D_RWKV:3 * D_RWKV]
    ld = dr[:, 0:D_RWKV]
    kd = dr[:, D_RWKV:2 * D_RWKV]
    b = dr[:, 2 * D_RWKV:3 * D_RWKV]

    cum = _cumsum_rows(ld, m["reverse"], L)
    tot = cum[0:1, :] if m["reverse"] else cum[L - 1:L, :]
    kkt = kk * jnp.exp(cum - ld)
    rt = r * jnp.exp(cum)
    iw = jnp.exp(-cum)
    kh = kd * iw
    bh = b * iw
    wl = jnp.exp(tot - cum)
    khw = kd * wl
    bhw = b * wl
    wtot = jnp.exp(tot)

    def pair_chain(p, res):
        sl = slice(p * LANES, (p + 1) * LANES)
        kkt_p, rt_p, kh_p, bh_p = kkt[:, sl], rt[:, sl], kh[:, sl], bh[:, sl]
        v_p, khw_p, bhw_p = v[:, sl], khw[:, sl], bhw[:, sl]
        lhs = jnp.concatenate([kkt_p, rt_p], axis=0)
        sc = _mm_nt(lhs, jnp.concatenate([bd(bh_p), bd(kh_p)], axis=0))
        yield
        sb = sc[:, :LANES]
        sk = sc[:, LANES:]
        a_ab = jnp.where(m["strict"], sb[:L], 0.0)
        a_rb = jnp.where(m["incl"], sb[L:], 0.0)
        a_ak = jnp.where(m["strict"], sk[:L], 0.0)
        a_rk = jnp.where(m["incl"], sk[L:], 0.0)
        avk = _mm(jnp.concatenate([a_ak, a_rk], axis=0), bd(v_p))
        t = m["eye"] - jnp.where(m["same2"], a_ab, 0.0)
        g = 2
        while g < L:
            w = _mm(t, bd(jnp.where(m["off%d" % g], a_ab, 0.0)))
            yield
            t = t - _mm(w, bd(t))
            yield
            g *= 2
        pu = _mm(t, jnp.concatenate([bd(kkt_p), bd(avk[:L])], axis=1))
        yield
        pm = pu[:, :LANES]
        u0 = pu[:, LANES:]
        qy = _mm(a_rb, jnp.concatenate([bd(pm), bd(u0)], axis=1))
        mc = jnp.where(m["diag_blocks"], _mm_tn(pm, bhw_p), 0.0)
        gg = pd(_mm_tn(jnp.concatenate([v_p, u0], axis=0),
                       jnp.concatenate([khw_p, -bhw_p], axis=0)))
        q = rt_p - qy[:, :LANES]
        y0 = avk[L:] - qy[:, LANES:]
        res[p] = (q, y0, mc, gg, wtot[:, sl])

    res = [None] * N_PAIRS
    return res, [pair_chain(p, res) for p in range(N_PAIRS)]


def _round_robin(chains):
    chains = list(chains)
    while chains:
        alive = []
        for ch in chains:
            try:
                next(ch)
                alive.append(ch)
            except StopIteration:
                pass
        chains = alive


def _wkv_kernel(shf_ref, shb_ref, drf_ref, drb_ref, s0_ref, yf_ref, yb_ref, sout_ref, s_ref,
                *, L, cps, nsteps):
    j = pl.program_id(1)

    @pl.when(j == 0)
    def _():
        s_ref[...] = s0_ref[0]

    io = ((shf_ref, drf_ref, yf_ref), (shb_ref, drb_ref, yb_ref))
    order = (list(range(cps)), list(range(cps - 1, -1, -1)))
    results, chains = {}, []
    for d, (sh_ref, dr_ref, _) in enumerate(io):
        m = _wkv_masks(d == 1, L)
        for c in order[d]:
            rows = slice(c * L, (c + 1) * L)
            results[d, c], ch = _wkv_precompute(sh_ref[0, rows, :], dr_ref[0, 0, rows, :], m, L)
            chains += ch
    _round_robin(chains)

    left = lax.broadcasted_iota(jnp.int32, (L, LANES), 1) < HEAD_DIM
    state = [[s_ref[d, p] for p in range(N_PAIRS)] for d in range(2)]
    for i in range(cps):
        old = [[None] * N_PAIRS for _ in range(2)]
        for d in range(2):
            for p in range(N_PAIRS):
                _, _, mc, gg, wtot = results[d, order[d][i]][p]
                s0 = state[d][p]
                old[d][p] = s0
                state[d][p] = s0 * wtot - _mm(s0, mc) + gg
        for d in range(2):
            c = order[d][i]
            for p in range(N_PAIRS):
                q, y0 = results[d, c][p][:2]
                s0 = old[d][p]
                s0_bd = jnp.concatenate([jnp.where(left, s0, 0.0), jnp.where(left, 0.0, s0)], axis=0)
                io[d][2][0, c * L:(c + 1) * L, p * LANES:(p + 1) * LANES] = _mm_nt(q, s0_bd) + y0
    for d in range(2):
        for p in range(N_PAIRS):
            s_ref[d, p] = state[d][p]

    @pl.when(j == nsteps - 1)
    def _():
        sout_ref[0] = s_ref[...]


def _wkv(sh, dr, s0):
    bsz, seq, c3 = sh.shape
    L = CHUNK
    assert L == HEAD_DIM
    cps = CHUNKS_PER_STEP
    tb = cps * L
    assert seq % tb == 0
    ns = seq // tb
    kern = functools.partial(_wkv_kernel, L=L, cps=cps, nsteps=ns)
    st_shape = (1, 2, N_PAIRS, HEAD_DIM, LANES)
    return pl.pallas_call(
        kern,
        grid=(bsz, ns),
        in_specs=[pl.BlockSpec((1, tb, c3), lambda b, j: (b, j, 0)),
                  pl.BlockSpec((1, tb, c3), lambda b, j: (b, ns - 1 - j, 0)),
                  pl.BlockSpec((1, 1, tb, c3), lambda b, j: (0, b, j, 0)),
                  pl.BlockSpec((1, 1, tb, c3), lambda b, j: (1, b, ns - 1 - j, 0)),
                  pl.BlockSpec(st_shape, lambda b, j: (b, 0, 0, 0, 0))],
        out_specs=[pl.BlockSpec((1, tb, D_RWKV), lambda b, j: (b, j, 0)),
                   pl.BlockSpec((1, tb, D_RWKV), lambda b, j: (b, ns - 1 - j, 0)),
                   pl.BlockSpec(st_shape, lambda b, j: (b, 0, 0, 0, 0))],
        out_shape=[jax.ShapeDtypeStruct((bsz, seq, D_RWKV), F32),
                   jax.ShapeDtypeStruct((bsz, seq, D_RWKV), F32),
                   jax.ShapeDtypeStruct((bsz, 2, N_PAIRS, HEAD_DIM, LANES), F32)],
        scratch_shapes=[pltpu.VMEM((2, N_PAIRS, HEAD_DIM, LANES), F32)],
        compiler_params=_params(2),
        name="wkv",
    )(sh, sh, dr, dr, s0)


def _ffn_body(x, mod_ref, gn_ref, wup_ref, cw_ref, wdn_ref, tm, row_len):
    h = _norm_mod(x, gn_ref[2:3, :], mod_ref[0, 3:4, :], mod_ref[0, 4:5, :]).astype(BF16)
    first, last = _row_edges(tm, row_len)
    fc = D_FF // FF_CHUNKS
    yo = None
    for i in range(FF_CHUNKS):
        a = jnp.dot(h, wup_ref[:, i * fc:(i + 1) * fc], preferred_element_type=F32)
        gate = jnp.dot(h, wup_ref[:, D_FF + i * fc:D_FF + (i + 1) * fc], preferred_element_type=F32)
        c = _dwconv3(a, cw_ref[:, i * fc:(i + 1) * fc], first, last, tm)
        u = (c * _sigmoid(c)) * gate
        part = jnp.dot(u.astype(BF16), wdn_ref[i * fc:(i + 1) * fc, :], preferred_element_type=F32)
        yo = part if yo is None else yo + part
    return x + mod_ref[0, 5:6, :] * _rms(yo, gn_ref[3:4, :])


def _ffn_specs():
    return [_const_spec((D_MODEL, 2 * D_FF)), _const_spec((3, D_FF)), _const_spec((D_FF, D_MODEL))]


def _token_tiling(x, mod, row_len):
    bsz, seq, _ = x.shape
    tm = min(TM_MIX, seq)
    assert seq % tm == 0 and tm % row_len == 0 and row_len & (row_len - 1) == 0
    assert D_FF % (FF_CHUNKS * LANES) == 0
    mod_map = (lambda b, j: (b, 0, 0)) if mod.shape[0] == bsz else (lambda b, j: (0, 0, 0))
    return bsz, seq // tm, tm, pl.BlockSpec((1, 6, D_MODEL), mod_map)


def _even_out_kernel(x_ref, yf_ref, yb_ref, mg_ref, mod_ref, gn_ref, vec_ref, e_ref, band_ref,
                     icnt_ref, pw_ref, wout_ref, wup_ref, cw_ref, wdn_ref, o_ref, *, tm, pt, row_len):
    y = yf_ref[0] + yb_ref[0]
    inv_n = 1.0 / HEAD_DIM
    mean = _split_dot_rhs01(y, e_ref[...]) * inv_n
    yc = y - mean
    var = _split_dot_rhs01(yc * yc, e_ref[...]) * inv_n
    yn = yc * lax.rsqrt(var + GN_EPS) * vec_ref[7:8, :] + vec_ref[8:9, :]
    y_rwkv = (yn + mg_ref[0, :, 0:D_RWKV]) * mg_ref[0, :, D_RWKV:2 * D_RWKV]

    pp = mg_ref[0, :, 2 * D_RWKV:3 * D_RWKV]
    outs = []
    for gi in range(len(POOL_WINDOWS)):
        sl = slice(gi * POOL_GROUP, (gi + 1) * POOL_GROUP)
        xg = pp[:, sl]
        wsum = jnp.concatenate([_split_dot_lhs01(band_ref[gi], xg[s * pt:(s + 1) * pt])
                                for s in range(tm // pt)], axis=0)
        icnt = jnp.concatenate([icnt_ref[:, sl]] * (tm // pt), axis=0)
        dg = wsum * icnt - xg
        outs.append(_bdot(dg, pw_ref[gi]))
    y_pool = jnp.concatenate(outs, axis=-1) * vec_ref[9:10, :]

    cat = jnp.concatenate([y_rwkv, y_pool], axis=-1)
    yo = _bdot(cat, wout_ref[...])
    x1 = x_ref[0] + mod_ref[0, 2:3, :] * _rms(yo, gn_ref[1:2, :])
    o_ref[0] = _ffn_body(x1, mod_ref, gn_ref, wup_ref, cw_ref, wdn_ref, tm, row_len)


def _even_out_ffn(x, yf, yb, mg, mod, gn, ew, pc, fw, row_len):
    bsz, nt, tm, mod_spec = _token_tiling(x, mod, row_len)
    c3 = 3 * D_RWKV
    tok = lambda w: pl.BlockSpec((1, tm, w), lambda b, j: (b, j, 0))
    pt = pc["band"].shape[1]
    assert tm % pt == 0
    return pl.pallas_call(
        functools.partial(_even_out_kernel, tm=tm, pt=pt, row_len=row_len),
        grid=(bsz, nt),
        in_specs=[tok(D_MODEL), tok(D_RWKV), tok(D_RWKV), tok(c3), mod_spec,
                  _const_spec((4, D_MODEL)),
                  _const_spec((16, D_RWKV)),
                  _const_spec((D_RWKV, D_RWKV)),
                  _const_spec((4, pt, pt)),
                  _const_spec((pt, D_POOL)),
                  _const_spec((4, POOL_GROUP, POOL_GROUP)),
                  _const_spec((D_MODEL, D_MODEL))] + _ffn_specs(),
        out_specs=tok(D_MODEL),
        out_shape=jax.ShapeDtypeStruct(x.shape, F32),
        compiler_params=_params(2),
        name="even_out_ffn",
    )(x, yf, yb, mg, mod, gn, ew["vec"], ew["seg"], pc["band"], pc["icnt"], ew["pool_w"], ew["w_out"],
      fw["w_up"], fw["conv"], fw["w_down"])


def _odd_kernel(x_ref, mod_ref, gn_ref, win_ref, cw_ref, wout_ref, fup_ref, fcw_ref, fdn_ref, o_ref,
                *, tm, row_len):
    x = x_ref[0]
    h = _norm_mod(x, gn_ref[0:1, :], mod_ref[0, 0:1, :], mod_ref[0, 1:2, :])
    p = jnp.dot(h.astype(BF16), win_ref[...], preferred_element_type=F32)
    bg = p[:, 0:D_MODEL]
    cg = p[:, D_MODEL:2 * D_MODEL]
    u = p[:, 2 * D_MODEL:3 * D_MODEL]
    first, last = _row_edges(tm, row_len)
    z = bg * _dwconv3(cg * u, cw_ref[...], first, last, tm)
    yo = _bdot(z, wout_ref[...])
    x1 = x + mod_ref[0, 2:3, :] * _rms(yo, gn_ref[1:2, :])
    o_ref[0] = _ffn_body(x1, mod_ref, gn_ref, fup_ref, fcw_ref, fdn_ref, tm, row_len)


def _odd_mix_ffn(x, mod, gn, ow, fw, row_len):
    bsz, nt, tm, mod_spec = _token_tiling(x, mod, row_len)
    tok = pl.BlockSpec((1, tm, D_MODEL), lambda b, j: (b, j, 0))
    return pl.pallas_call(
        functools.partial(_odd_kernel, tm=tm, row_len=row_len),
        grid=(bsz, nt),
        in_specs=[tok, mod_spec,
                  _const_spec((4, D_MODEL)),
                  _const_spec((D_MODEL, 3 * D_MODEL)),
                  _const_spec((3, D_MODEL)),
                  _const_spec((D_MODEL, D_MODEL))] + _ffn_specs(),
        out_specs=tok,
        out_shape=jax.ShapeDtypeStruct(x.shape, F32),
        compiler_params=_params(2),
        name="odd_mix_ffn",
    )(x, mod, gn, ow["w_in"], ow["conv"], ow["w_out"], fw["w_up"], fw["conv"], fw["w_down"])


def _pool_consts(tm, row_len):
    pos = np.arange(tm)
    col = pos % row_len
    same_row = (pos[:, None] // row_len) == (pos[None, :] // row_len)
    band = np.zeros((len(POOL_WINDOWS), tm, tm), np.float32)
    icnt = np.zeros((tm, D_POOL), np.float32)
    for gi, win in enumerate(POOL_WINDOWS):
        lo = np.clip(col - win // 2, 0, row_len)
        hi = np.clip(col + win // 2, 0, row_len)
        inside = (col[None, :] >= lo[:, None]) & (col[None, :] < hi[:, None]) & same_row
        band[gi] = inside.astype(np.float32)
        icnt[:, gi * POOL_GROUP:(gi + 1) * POOL_GROUP] = (1.0 / (hi - lo).astype(np.float32))[:, None]
    return {"band": jnp.asarray(band, BF16), "icnt": jnp.asarray(icnt, F32)}


def _even_weights(i, ev_w_in, ev_w_out, ev_mu, ev_w0, ev_w_up, ev_a0, ev_a_up, ev_g_up, ev_k_k,
                  ev_k_a, ev_r_k, ev_gn_w, ev_gn_b, ev_pool_w, ev_pool_scale):
    o_lw = 3 * D_RWKV
    o_la = o_lw + 2 * D_DECAY_LORA
    o_lg = o_la + 2 * D_AAA_LORA
    o_pool = o_lg + D_GATE_LORA

    def repack(a, with_pool):
        z = lambda n: jnp.zeros(a.shape[:-1] + (n,), a.dtype)
        parts = [a[..., :o_lw], a[..., o_lw:o_la], z(LANES - 2 * D_DECAY_LORA),
                 a[..., o_la:o_lg], a[..., o_lg:o_pool], z(LANES - D_GATE_LORA)]
        if with_pool:
            parts.append(a[..., o_pool:])
        return jnp.concatenate(parts, axis=-1)

    def per_dir(w):
        rows = jnp.concatenate([jnp.pad(w[0], ((0, 0), (0, D_RWKV))), jnp.pad(w[1], ((0, 0), (D_RWKV, 0)))], axis=0)
        return jnp.pad(rows, ((0, LANES - rows.shape[0]), (0, 0)))

    wup = per_dir(ev_w_up[i])
    aup = per_dir(ev_a_up[i])
    gup = jnp.pad(ev_g_up[i], ((0, LANES - D_GATE_LORA), (0, 0)))
    row = lambda a: a.reshape(1, D_RWKV)
    vec = jnp.concatenate([ev_w0[i], ev_a0[i], row(ev_k_k[i]), row(ev_k_a[i]), row(ev_r_k[i]), row(ev_gn_w[i]),
                           row(ev_gn_b[i]), row(ev_pool_scale[i]), jnp.zeros((6, D_RWKV), F32)], axis=0)
    head = np.arange(D_RWKV) // HEAD_DIM
    seg = jnp.asarray(head[:, None] == head[None, :], BF16)
    return {"w_in": repack(ev_w_in[i], True).astype(BF16), "mu": repack(ev_mu[i], False),
            "wup": wup.astype(BF16), "aup": aup.astype(BF16), "gup": gup.astype(BF16),
            "vec": vec, "seg": seg, "pool_w": ev_pool_w[i].astype(BF16),
            "w_out": ev_w_out[i].astype(BF16)}


def kernel(x, c, ctx, c_ctx, w_mod, b_mod, norm_g, ffn_w_up, ffn_conv, ffn_w_down, ev_w_in, ev_w_out,
           ev_mu, ev_w0, ev_w_up, ev_a0, ev_a_up, ev_g_up, ev_k_k, ev_k_a, ev_r_k, ev_gn_w, ev_gn_b,
           ev_pool_w, ev_pool_scale, od_w_in, od_conv, od_w_out):
    bsz, seq, _ = x.shape
    ctx_len = ctx.shape[1]
    assert TM_MIX % POOL_TILE == 0 and POOL_TILE % GRID_W == 0
    assert ctx_len <= min(TM_MIX, TM_EVEN_IN) and ctx_len % (CHUNK * CHUNKS_PER_STEP) == 0 and bsz < 16

    cv = jnp.concatenate([c, c_ctx[None], jnp.zeros((15 - bsz, D_MODEL), F32)], axis=0)
    mods = _mods(cv, w_mod, b_mod)
    pc_lat = _pool_consts(POOL_TILE, GRID_W)
    pc_ctx = _pool_consts(ctx_len, ctx_len)
    ffn_w_up_b = ffn_w_up.astype(BF16)
    ffn_w_down_b = ffn_w_down.astype(BF16)
    od_w_in_b = od_w_in.astype(BF16)
    od_w_out_b = od_w_out.astype(BF16)

    for layer in range(DEPTH):
        i = layer // 2
        even = layer % 2 == 0
        ctx_later = any(jj % 2 == 0 for jj in range(layer + 1, DEPTH))
        mod = mods[layer, :bsz].reshape(bsz, 6, D_MODEL)
        mod_c = mods[layer, bsz:bsz + 1].reshape(1, 6, D_MODEL)
        gn = norm_g[layer]
        fw = {"w_up": ffn_w_up_b[layer], "conv": ffn_conv[layer], "w_down": ffn_w_down_b[layer]}
        if even:
            ew = _even_weights(i, ev_w_in, ev_w_out, ev_mu, ev_w0, ev_w_up, ev_a0, ev_a_up, ev_g_up,
                               ev_k_k, ev_k_a, ev_r_k, ev_gn_w, ev_gn_b, ev_pool_w, ev_pool_scale)
            sh_c, dr_c, mg_c = _even_in(ctx, mod_c, gn, ew)
            sh_l, dr_l, mg_l = _even_in(x, mod, gn, ew)
            zero = jnp.zeros((bsz, 2, N_PAIRS, HEAD_DIM, LANES), F32)
            yf_c, yb_c, s_c = _wkv(sh_c, dr_c, zero)
            yf_l, yb_l, _ = _wkv(sh_l, dr_l, s_c)
            x = _even_out_ffn(x, yf_l, yb_l, mg_l, mod, gn, ew, pc_lat, fw, GRID_W)
            if ctx_later:
                ctx = _even_out_ffn(ctx, yf_c, yb_c, mg_c, mod_c, gn, ew, pc_ctx, fw, ctx_len)
        else:
            ow = {"w_in": od_w_in_b[i], "conv": od_conv[i], "w_out": od_w_out_b[i]}
            x = _odd_mix_ffn(x, mod, gn, ow, fw, GRID_W)
            if ctx_later:
                ctx = _odd_mix_ffn(ctx, mod_c, gn, ow, fw, ctx_len)
    return x
```

```python
import functools
import math

import numpy as np
import jax
import jax.numpy as jnp
from jax import lax
from jax.experimental import pallas as pl
from jax.experimental.pallas import tpu as pltpu

D_MODEL = 1024
DEPTH = 4
GRID_W = 64
HEAD_DIM = 64
N_HEADS = 8
D_RWKV = N_HEADS * HEAD_DIM
D_POOL = D_MODEL - D_RWKV
POOL_WINDOWS = (2, 4, 8, 16)
POOL_GROUP = D_POOL // len(POOL_WINDOWS)
D_DECAY_LORA = 32
D_AAA_LORA = 64
D_GATE_LORA = 96
D_FF = 2816
RMS_EPS = 1e-6
GN_EPS = 64e-5
DECAY_SCALE = math.exp(-0.5)

LANES = 128
SUBLANES = 8
MXU_DIM = 256
N_PAIRS = N_HEADS // 2

C_LW = 3 * D_RWKV
C_LA = C_LW + LANES
C_LG = C_LA + LANES
C_SHIFT = C_LG + LANES
C_POOL = C_SHIFT
C_EVEN = C_POOL + D_POOL

HALO = SUBLANES
CHUNK = 64
CHUNKS_PER_STEP = 4
TM_EVEN_IN = 256
SUB_EVEN_IN = 256
TM_MIX = 512
SUB_MIX = 256
POOL_TILE = 256
FF_SLABS = ((0, 1536), (1536, D_FF))

F32 = jnp.float32
BF16 = jnp.bfloat16
HIGHEST = lax.Precision.HIGHEST
VMEM_LIMIT = 56 * 1024 * 1024


def _sigmoid(x):
    return 1.0 / (1.0 + jnp.exp(-x))


def _rms(x, g):
    ms = jnp.mean(x * x, axis=-1, keepdims=True)
    return x * lax.rsqrt(ms + RMS_EPS) * g


def _norm_mod(x, g, shift, scale):
    return _rms(x, g) * (1.0 + scale) + shift


def _bdot(a, b):
    return jnp.dot(a.astype(BF16), b.astype(BF16), preferred_element_type=F32)


def _head_sums(x, e):
    m, width = x.shape
    w = e.shape[0]
    xs = jnp.concatenate([x[:, i * w:(i + 1) * w] for i in range(width // w)], axis=0)
    hi = xs.astype(BF16)
    lo = (xs - hi.astype(F32)).astype(BF16)
    s = jnp.dot(hi, e, preferred_element_type=F32) + jnp.dot(lo, e, preferred_element_type=F32)
    return jnp.concatenate([s[i * m:(i + 1) * m] for i in range(width // w)], axis=1)


def _split_dot_lhs01(e, x):
    hi = x.astype(BF16)
    lo = (x - hi.astype(F32)).astype(BF16)
    return (jnp.dot(e, hi, preferred_element_type=F32)
            + jnp.dot(e, lo, preferred_element_type=F32))


def _row_edges(tm, row_len):
    pos = lax.broadcasted_iota(jnp.int32, (tm, 1), 0) & (row_len - 1)
    return pos == 0, pos == row_len - 1


def _dwconv3(x, w, first, last, tm):
    prev = jnp.where(first, 0.0, pltpu.roll(x, 1, 0))
    nxt = jnp.where(last, 0.0, pltpu.roll(x, tm - 1, 0))
    return prev * w[0:1, :] + x * w[1:2, :] + nxt * w[2:3, :]


def _const_spec(shape):
    nd = len(shape)
    return pl.BlockSpec(shape, lambda *_: (0,) * nd, pipeline_mode=pl.Buffered(1))


def _params(n_axes):
    return pltpu.CompilerParams(dimension_semantics=("arbitrary",) * n_axes,
                                vmem_limit_bytes=VMEM_LIMIT)


def _mods_kernel(cv_ref, w_ref, b_ref, o_ref):
    cv = cv_ref[...]
    s = cv * _sigmoid(cv)
    o_ref[0] = jnp.dot(s, w_ref[0], preferred_element_type=F32, precision=HIGHEST) + b_ref[0]


def _mods(cv, w_mod, b_mod):
    tn = 1536
    n6 = 6 * D_MODEL
    return pl.pallas_call(
        _mods_kernel,
        grid=(DEPTH, n6 // tn),
        in_specs=[pl.BlockSpec((16, D_MODEL), lambda l, n: (0, 0)),
                  pl.BlockSpec((1, D_MODEL, tn), lambda l, n: (l, 0, n)),
                  pl.BlockSpec((1, 1, tn), lambda l, n: (l, 0, n))],
        out_specs=pl.BlockSpec((1, 16, tn), lambda l, n: (l, 0, n)),
        out_shape=jax.ShapeDtypeStruct((DEPTH, 16, n6), F32),
        compiler_params=_params(2),
        name="mods",
    )(cv, w_mod, b_mod.reshape(DEPTH, 1, n6))


def _even_in_chain(s, n_sub, tm, j, nt, x_ref, xp_ref, xn_ref, mod_ref, gn_ref, win_ref, mu_ref, wup_ref,
                   aup_ref, gup_ref, vec_ref, e_ref, sh_ref, dr_ref, mg_ref):
    te = tm + 2 * HALO
    rows = slice(s * tm, (s + 1) * tm)
    top = xp_ref[0] if s == 0 else x_ref[0, s * tm - HALO:s * tm, :]
    bot = xn_ref[0] if s == n_sub - 1 else x_ref[0, (s + 1) * tm:(s + 1) * tm + HALO, :]
    xe = jnp.concatenate([top, x_ref[0, rows, :], bot], axis=0)
    h32 = _norm_mod(xe, gn_ref[0:1, :], mod_ref[0, 0:1, :], mod_ref[0, 1:2, :])
    h = h32.astype(BF16)

    def proj(c0, c1):
        return jnp.dot(h, win_ref[:, c0:c1], preferred_element_type=F32)

    def shifted(pb, c0):
        c1 = c0 + pb.shape[1]
        pc = pb[HALO:tm + HALO]
        p_top = jnp.where(j == 0, 0.0, pb[:HALO]) if s == 0 else pb[:HALO]
        p_bot = jnp.where(j == nt - 1, 0.0, pb[tm + HALO:]) if s == n_sub - 1 else pb[tm + HALO:]
        pe = jnp.concatenate([p_top, pc, p_bot], axis=0)
        prev = pltpu.roll(pe, 1, 0)[HALO:tm + HALO]
        nxt = pltpu.roll(pe, te - 1, 0)[HALO:tm + HALO]
        mu0 = mu_ref[0:1, c0:c1]
        mu1 = mu_ref[1:2, c0:c1]
        return pc * (1.0 - mu0 - mu1) + mu0 * prev + mu1 * nxt

    p_k = proj(D_RWKV, 2 * D_RWKV)
    p_l = proj(C_LW, C_SHIFT)
    p_r = proj(0, D_RWKV)
    p_v = proj(2 * D_RWKV, 3 * D_RWKV)
    p_pool = jnp.dot(h32[HALO:tm + HALO].astype(BF16), win_ref[:, C_POOL:C_EVEN], preferred_element_type=F32)
    yield
    k = shifted(p_k, D_RWKV)
    lora = shifted(p_l, C_LW)
    wd = _bdot(jnp.tanh(lora[:, 0:LANES]), wup_ref[...])
    la = _bdot(lora[:, C_LA - C_LW:C_LA - C_LW + LANES], aup_ref[...])
    g = _bdot(_sigmoid(lora[:, C_LG - C_LW:C_LG - C_LW + LANES]), gup_ref[...])
    r = shifted(p_r, 0)
    v = shifted(p_v, 2 * D_RWKV)
    k_k = vec_ref[4:5, :]
    k_a = vec_ref[5:6, :]
    r_k = vec_ref[6:7, :]
    kkr = k * k_k
    nrm = _head_sums(kkr * kkr, e_ref[...])
    yield
    kk = kkr * lax.rsqrt(jnp.maximum(nrm, 1e-24))
    kd_sum = None
    for d in range(2):
        wdd = vec_ref[d:d + 1, :] + wd[:, d * D_RWKV:(d + 1) * D_RWKV]
        ld = -DECAY_SCALE * _sigmoid(wdd)
        a = _sigmoid(vec_ref[2 + d:3 + d, :] + la[:, d * D_RWKV:(d + 1) * D_RWKV])
        kd = k * (1.0 + (a - 1.0) * k_a)
        dr_ref[d, 0, rows, 0:D_RWKV] = ld
        dr_ref[d, 0, rows, D_RWKV:2 * D_RWKV] = kd
        dr_ref[d, 0, rows, 2 * D_RWKV:3 * D_RWKV] = kk * a
        kd_sum = kd if kd_sum is None else kd_sum + kd
    coef = _head_sums(r * kd_sum * r_k, e_ref[...])
    yield
    sh_ref[0, rows, 0:D_RWKV] = r
    sh_ref[0, rows, D_RWKV:2 * D_RWKV] = v
    sh_ref[0, rows, 2 * D_RWKV:3 * D_RWKV] = kk
    mg_ref[0, rows, 0:D_RWKV] = coef * v
    mg_ref[0, rows, D_RWKV:2 * D_RWKV] = g
    mg_ref[0, rows, 2 * D_RWKV:3 * D_RWKV] = p_pool


def _even_in_kernel(*refs, sub, n_sub, nt):
    j = pl.program_id(1)
    _round_robin([_even_in_chain(s, n_sub, sub, j, nt, *refs) for s in range(n_sub)])


def _even_in(x, mod, gn, ew):
    bsz, seq, _ = x.shape
    tm = min(TM_EVEN_IN, seq)
    sub = min(SUB_EVEN_IN, tm)
    assert seq % tm == 0 and tm % sub == 0
    nt = seq // tm
    hb = tm // HALO
    nhb = seq // HALO
    c3 = 3 * D_RWKV
    kern = functools.partial(_even_in_kernel, sub=sub, n_sub=tm // sub, nt=nt)
    mod_map = (lambda b, j: (b, 0, 0)) if mod.shape[0] == bsz else (lambda b, j: (0, 0, 0))
    return pl.pallas_call(
        kern,
        grid=(bsz, nt),
        in_specs=[pl.BlockSpec((1, tm, D_MODEL), lambda b, j: (b, j, 0)),
                  pl.BlockSpec((1, HALO, D_MODEL), lambda b, j: (b, jnp.maximum(j * hb - 1, 0), 0)),
                  pl.BlockSpec((1, HALO, D_MODEL), lambda b, j: (b, jnp.minimum((j + 1) * hb, nhb - 1), 0)),
                  pl.BlockSpec((1, 6, D_MODEL), mod_map),
                  _const_spec((4, D_MODEL)),
                  _const_spec((D_MODEL, C_EVEN)),
                  _const_spec((2, C_SHIFT)),
                  _const_spec((LANES, 2 * D_RWKV)),
                  _const_spec((LANES, 2 * D_RWKV)),
                  _const_spec((LANES, D_RWKV)),
                  _const_spec((16, D_RWKV)),
                  _const_spec((MXU_DIM, MXU_DIM))],
        out_specs=[pl.BlockSpec((1, tm, c3), lambda b, j: (b, j, 0)),
                   pl.BlockSpec((2, 1, tm, c3), lambda b, j: (0, b, j, 0)),
                   pl.BlockSpec((1, tm, c3), lambda b, j: (b, j, 0))],
        out_shape=[jax.ShapeDtypeStruct((bsz, seq, c3), F32),
                   jax.ShapeDtypeStruct((2, bsz, seq, c3), F32),
                   jax.ShapeDtypeStruct((bsz, seq, c3), F32)],
        compiler_params=_params(2),
        name="even_in",
    )(x, x, x, mod, gn, ew["w_in"], ew["mu"], ew["wup"], ew["aup"], ew["gup"], ew["vec"], ew["seg"])


def _mm(a, b):
    return jnp.dot(a.astype(BF16), b.astype(BF16), preferred_element_type=F32)


def _mm_nt(a, b):
    return lax.dot_general(a.astype(BF16), b.astype(BF16), (((1,), (1,)), ((), ())),
                           preferred_element_type=F32)


def _mm_tn(a, b):
    return lax.dot_general(a.astype(BF16), b.astype(BF16), (((0,), (0,)), ((), ())),
                           preferred_element_type=F32)


def _cumsum_rows(x, reverse, n):
    rows = lax.broadcasted_iota(jnp.int32, (n, 1), 0)
    s = 1
    while s < n:
        if s < SUBLANES:
            if reverse:
                shifted = jnp.where(rows < n - s, pltpu.roll(x, n - s, 0), 0.0)
            else:
                shifted = jnp.where(rows >= s, pltpu.roll(x, s, 0), 0.0)
        else:
            zeros = jnp.zeros((s, x.shape[1]), x.dtype)
            shifted = (jnp.concatenate([x[s:], zeros], axis=0) if reverse
                       else jnp.concatenate([zeros, x[:n - s]], axis=0))
        x = x + shifted
        s *= 2
    return x


def _wkv_masks(reverse, L):
    row = lax.broadcasted_iota(jnp.int32, (L, LANES), 0)
    lane = lax.broadcasted_iota(jnp.int32, (L, LANES), 1)
    colp = lane & (HEAD_DIM - 1)
    row2 = lax.broadcasted_iota(jnp.int32, (2 * L, LANES), 0)
    lane2 = lax.broadcasted_iota(jnp.int32, (2 * L, LANES), 1)
    blk = row ^ colp
    m = {
        "reverse": reverse,
        "left": lane < HEAD_DIM,
        "incl": (row <= colp) if reverse else (row >= colp),
        "strict": (row < colp) if reverse else (row > colp),
        "eye": (row == colp).astype(F32),
        "diag_blocks": (row2 < L) == (lane2 < HEAD_DIM),
        "same2": (blk >> 1) == 0,
    }
    g, lg = 2, 1
    while g < L:
        m["off%d" % g] = (blk >> lg) == 1
        g, lg = 2 * g, lg + 1
    return m


def _wkv_precompute(sh, dr, m, L):
    left = m["left"]

    def bd(x):
        return jnp.concatenate([jnp.where(left, x, 0.0), jnp.where(left, 0.0, x)], axis=0)

    def pd(y):
        return jnp.where(left, y[:L], y[L:])

    r = sh[:, 0:D_RWKV]
    v = sh[:, D_RWKV:2 * D_RWKV]
    kk = sh[:, 2 * D_RWKV:3 * D_RWKV]
    ld = dr[:, 0:D_RWKV]
    kd = dr[:, D_RWKV:2 * D_RWKV]
    b = dr[:, 2 * D_RWKV:3 * D_RWKV]

    cum = _cumsum_rows(ld, m["reverse"], L)
    tot = cum[0:1, :] if m["reverse"] else cum[L - 1:L, :]
    kkt = kk * jnp.exp(cum - ld)
    rt = r * jnp.exp(cum)
    iw = jnp.exp(-cum)
    kh = kd * iw
    bh = b * iw
    wl = jnp.exp(tot - cum)
    khw = kd * wl
    bhw = b * wl
    wtot = jnp.exp(tot)

    def pair_chain(p, res):
        sl = slice(p * LANES, (p + 1) * LANES)
        kkt_p, rt_p, kh_p, bh_p = kkt[:, sl], rt[:, sl], kh[:, sl], bh[:, sl]
        v_p, khw_p, bhw_p = v[:, sl], khw[:, sl], bhw[:, sl]
        lhs = jnp.concatenate([kkt_p, rt_p], axis=0)
        sc = _mm_nt(lhs, jnp.concatenate([bd(bh_p), bd(kh_p)], axis=0))
        yield
        sb = sc[:, :LANES]
        sk = sc[:, LANES:]
        a_ab = jnp.where(m["strict"], sb[:L], 0.0)
        a_rb = jnp.where(m["incl"], sb[L:], 0.0)
        a_ak = jnp.where(m["strict"], sk[:L], 0.0)
        a_rk = jnp.where(m["incl"], sk[L:], 0.0)
        avk = _mm(jnp.concatenate([a_ak, a_rk], axis=0), bd(v_p))
        t = m["eye"] - jnp.where(m["same2"], a_ab, 0.0)
        g = 2
        while g < L:
            w = _mm(t, bd(jnp.where(m["off%d" % g], a_ab, 0.0)))
            yield
            t = t - _mm(w, bd(t))
            yield
            g *= 2
        pu = _mm(t, jnp.concatenate([bd(kkt_p), bd(avk[:L])], axis=1))
        yield
        pm = pu[:, :LANES]
        u0 = pu[:, LANES:]
        qy = _mm(a_rb, jnp.concatenate([bd(pm), bd(u0)], axis=1))
        mc = jnp.where(m["diag_blocks"], _mm_tn(pm, bhw_p), 0.0)
        gg = pd(_mm_tn(jnp.concatenate([v_p, u0], axis=0),
                       jnp.concatenate([khw_p, -bhw_p], axis=0)))
        q = rt_p - qy[:, :LANES]
        y0 = avk[L:] - qy[:, LANES:]
        res[p] = (q, y0, mc, gg, wtot[:, sl])

    res = [None] * N_PAIRS
    return res, [pair_chain(p, res) for p in range(N_PAIRS)]


def _round_robin(chains):
    chains = list(chains)
    while chains:
        alive = []
        for ch in chains:
            try:
                next(ch)
                alive.append(ch)
            except StopIteration:
                pass
        chains = alive


def _wkv_kernel(shf_ref, shb_ref, drf_ref, drb_ref, s0_ref, yf_ref, yb_ref, sout_ref, s_ref,
                *, L, cps, nsteps):
    j = pl.program_id(1)

    @pl.when(j == 0)
    def _():
        s_ref[...] = s0_ref[0]

    io = ((shf_ref, drf_ref, yf_ref), (shb_ref, drb_ref, yb_ref))
    order = (list(range(cps)), list(range(cps - 1, -1, -1)))
    results, chains = {}, []
    for d, (sh_ref, dr_ref, _) in enumerate(io):
        m = _wkv_masks(d == 1, L)
        for c in order[d]:
            rows = slice(c * L, (c + 1) * L)
            results[d, c], ch = _wkv_precompute(sh_ref[0, rows, :], dr_ref[0, 0, rows, :], m, L)
            chains += ch
    _round_robin(chains)

    left = lax.broadcasted_iota(jnp.int32, (L, LANES), 1) < HEAD_DIM
    state = [[s_ref[d, p] for p in range(N_PAIRS)] for d in range(2)]
    for i in range(cps):
        old = [[None] * N_PAIRS for _ in range(2)]
        for d in range(2):
            for p in range(N_PAIRS):
                _, _, mc, gg, wtot = results[d, order[d][i]][p]
                s0 = state[d][p]
                old[d][p] = s0
                state[d][p] = s0 * wtot - _mm(s0, mc) + gg
        for d in range(2):
            c = order[d][i]
            for p in range(N_PAIRS):
                q, y0 = results[d, c][p][:2]
                s0 = old[d][p]
                s0_bd = jnp.concatenate([jnp.where(left, s0, 0.0), jnp.where(left, 0.0, s0)], axis=0)
                io[d][2][0, c * L:(c + 1) * L, p * LANES:(p + 1) * LANES] = _mm_nt(q, s0_bd) + y0
    for d in range(2):
        for p in range(N_PAIRS):
            s_ref[d, p] = state[d][p]

    @pl.when(j == nsteps - 1)
    def _():
        sout_ref[0] = s_ref[...]


def _wkv(sh, dr, s0):
    bsz, seq, c3 = sh.shape
    L = CHUNK
    assert L == HEAD_DIM
    cps = CHUNKS_PER_STEP
    tb = cps * L
    assert seq % tb == 0
    ns = seq // tb
    kern = functools.partial(_wkv_kernel, L=L, cps=cps, nsteps=ns)
    st_shape = (1, 2, N_PAIRS, HEAD_DIM, LANES)
    return pl.pallas_call(
        kern,
        grid=(bsz, ns),
        in_specs=[pl.BlockSpec((1, tb, c3), lambda b, j: (b, j, 0)),
                  pl.BlockSpec((1, tb, c3), lambda b, j: (b, ns - 1 - j, 0)),
                  pl.BlockSpec((1, 1, tb, c3), lambda b, j: (0, b, j, 0)),
                  pl.BlockSpec((1, 1, tb, c3), lambda b, j: (1, b, ns - 1 - j, 0)),
                  pl.BlockSpec(st_shape, lambda b, j: (b, 0, 0, 0, 0))],
        out_specs=[pl.BlockSpec((1, tb, D_RWKV), lambda b, j: (b, j, 0)),
                   pl.BlockSpec((1, tb, D_RWKV), lambda b, j: (b, ns - 1 - j, 0)),
                   pl.BlockSpec(st_shape, lambda b, j: (b, 0, 0, 0, 0))],
        out_shape=[jax.ShapeDtypeStruct((bsz, seq, D_RWKV), F32),
                   jax.ShapeDtypeStruct((bsz, seq, D_RWKV), F32),
                   jax.ShapeDtypeStruct((bsz, 2, N_PAIRS, HEAD_DIM, LANES), F32)],
        scratch_shapes=[pltpu.VMEM((2, N_PAIRS, HEAD_DIM, LANES), F32)],
        compiler_params=_params(2),
        name="wkv",
    )(sh, sh, dr, dr, s0)


def _ffn_chain(x, mod_ref, gn_ref, wup_ref, cw_ref, wdn_ref, tm, row_len, store):
    h = _norm_mod(x, gn_ref[2:3, :], mod_ref[0, 3:4, :], mod_ref[0, 4:5, :]).astype(BF16)
    first, last = _row_edges(tm, row_len)
    yo = None
    for c0, c1 in FF_SLABS:
        a = jnp.dot(h, wup_ref[:, c0:c1], preferred_element_type=F32)
        gate = jnp.dot(h, wup_ref[:, D_FF + c0:D_FF + c1], preferred_element_type=F32)
        yield
        c = _dwconv3(a, cw_ref[:, c0:c1], first, last, tm)
        u = (c * _sigmoid(c)) * gate
        part = jnp.dot(u.astype(BF16), wdn_ref[c0:c1, :], preferred_element_type=F32)
        yield
        yo = part if yo is None else yo + part
    store(x + mod_ref[0, 5:6, :] * _rms(yo, gn_ref[3:4, :]))


def _ffn_specs():
    return [_const_spec((D_MODEL, 2 * D_FF)), _const_spec((3, D_FF)), _const_spec((D_FF, D_MODEL))]


def _token_tiling(x, mod, row_len):
    bsz, seq, _ = x.shape
    tm = min(TM_MIX, seq)
    sub = min(SUB_MIX, tm)
    assert seq % tm == 0 and tm % sub == 0 and sub % row_len == 0 and row_len & (row_len - 1) == 0
    assert all(c % MXU_DIM == 0 for c, _ in FF_SLABS) and FF_SLABS[-1][1] == D_FF and D_RWKV % MXU_DIM == 0
    mod_map = (lambda b, j: (b, 0, 0)) if mod.shape[0] == bsz else (lambda b, j: (0, 0, 0))
    return bsz, seq // tm, tm, sub, pl.BlockSpec((1, 6, D_MODEL), mod_map)


def _even_out_chain(rows, x_ref, yf_ref, yb_ref, mg_ref, mod_ref, gn_ref, vec_ref, e_ref, band_ref,
                    icnt_ref, pw_ref, wout_ref, wup_ref, cw_ref, wdn_ref, o_ref, tm, pt, row_len):
    y = yf_ref[0, rows, :] + yb_ref[0, rows, :]
    inv_n = 1.0 / HEAD_DIM
    mean = _head_sums(y, e_ref[...]) * inv_n
    yield
    yc = y - mean
    var = _head_sums(yc * yc, e_ref[...]) * inv_n
    yield
    yn = yc * lax.rsqrt(var + GN_EPS) * vec_ref[7:8, :] + vec_ref[8:9, :]
    y_rwkv = (yn + mg_ref[0, rows, 0:D_RWKV]) * mg_ref[0, rows, D_RWKV:2 * D_RWKV]

    pp = mg_ref[0, rows, 2 * D_RWKV:3 * D_RWKV]
    outs = []
    for gi in range(len(POOL_WINDOWS)):
        sl = slice(gi * POOL_GROUP, (gi + 1) * POOL_GROUP)
        xg = pp[:, sl]
        wsum = jnp.concatenate([_split_dot_lhs01(band_ref[gi], xg[s * pt:(s + 1) * pt])
                                for s in range(tm // pt)], axis=0)
        icnt = jnp.concatenate([icnt_ref[:, sl]] * (tm // pt), axis=0)
        dg = wsum * icnt - xg
        outs.append(_bdot(dg, pw_ref[gi]))
    yield
    y_pool = jnp.concatenate(outs, axis=-1) * vec_ref[9:10, :]

    cat = jnp.concatenate([y_rwkv, y_pool], axis=-1)
    yo = _bdot(cat, wout_ref[...])
    yield
    x1 = x_ref[0, rows, :] + mod_ref[0, 2:3, :] * _rms(yo, gn_ref[1:2, :])

    def store(val):
        o_ref[0, rows, :] = val

    yield from _ffn_chain(x1, mod_ref, gn_ref, wup_ref, cw_ref, wdn_ref, tm, row_len, store)


def _even_out_kernel(*refs, tm, sub, pt, row_len):
    _round_robin([_even_out_chain(slice(s * sub, (s + 1) * sub), *refs, sub, pt, row_len)
                  for s in range(tm // sub)])


def _even_out_ffn(x, yf, yb, mg, mod, gn, ew, pc, fw, row_len):
    bsz, nt, tm, sub, mod_spec = _token_tiling(x, mod, row_len)
    c3 = 3 * D_RWKV
    tok = lambda w: pl.BlockSpec((1, tm, w), lambda b, j: (b, j, 0))
    pt = pc["band"].shape[1]
    assert sub % pt == 0
    return pl.pallas_call(
        functools.partial(_even_out_kernel, tm=tm, sub=sub, pt=pt, row_len=row_len),
        grid=(bsz, nt),
        in_specs=[tok(D_MODEL), tok(D_RWKV), tok(D_RWKV), tok(c3), mod_spec,
                  _const_spec((4, D_MODEL)),
                  _const_spec((16, D_RWKV)),
                  _const_spec((MXU_DIM, MXU_DIM)),
                  _const_spec((4, pt, pt)),
                  _const_spec((pt, D_POOL)),
                  _const_spec((4, POOL_GROUP, POOL_GROUP)),
                  _const_spec((D_MODEL, D_MODEL))] + _ffn_specs(),
        out_specs=tok(D_MODEL),
        out_shape=jax.ShapeDtypeStruct(x.shape, F32),
        compiler_params=_params(2),
        name="even_out_ffn",
    )(x, yf, yb, mg, mod, gn, ew["vec"], ew["seg"], pc["band"], pc["icnt"], ew["pool_w"], ew["w_out"],
      fw["w_up"], fw["conv"], fw["w_down"])


def _odd_chain(rows, x_ref, mod_ref, gn_ref, win_ref, cw_ref, wout_ref, fup_ref, fcw_ref, fdn_ref, o_ref,
               tm, row_len):
    x = x_ref[0, rows, :]
    h = _norm_mod(x, gn_ref[0:1, :], mod_ref[0, 0:1, :], mod_ref[0, 1:2, :])
    p = jnp.dot(h.astype(BF16), win_ref[...], preferred_element_type=F32)
    yield
    bg = p[:, 0:D_MODEL]
    cg = p[:, D_MODEL:2 * D_MODEL]
    u = p[:, 2 * D_MODEL:3 * D_MODEL]
    first, last = _row_edges(tm, row_len)
    z = bg * _dwconv3(cg * u, cw_ref[...], first, last, tm)
    yo = _bdot(z, wout_ref[...])
    yield
    x1 = x + mod_ref[0, 2:3, :] * _rms(yo, gn_ref[1:2, :])

    def store(val):
        o_ref[0, rows, :] = val

    yield from _ffn_chain(x1, mod_ref, gn_ref, fup_ref, fcw_ref, fdn_ref, tm, row_len, store)


def _odd_kernel(*refs, tm, sub, row_len):
    _round_robin([_odd_chain(slice(s * sub, (s + 1) * sub), *refs, sub, row_len) for s in range(tm // sub)])


def _odd_mix_ffn(x, mod, gn, ow, fw, row_len):
    bsz, nt, tm, sub, mod_spec = _token_tiling(x, mod, row_len)
    tok = pl.BlockSpec((1, tm, D_MODEL), lambda b, j: (b, j, 0))
    return pl.pallas_call(
        functools.partial(_odd_kernel, tm=tm, sub=sub, row_len=row_len),
        grid=(bsz, nt),
        in_specs=[tok, mod_spec,
                  _const_spec((4, D_MODEL)),
                  _const_spec((D_MODEL, 3 * D_MODEL)),
                  _const_spec((3, D_MODEL)),
                  _const_spec((D_MODEL, D_MODEL))] + _ffn_specs(),
        out_specs=tok,
        out_shape=jax.ShapeDtypeStruct(x.shape, F32),
        compiler_params=_params(2),
        name="odd_mix_ffn",
    )(x, mod, gn, ow["w_in"], ow["conv"], ow["w_out"], fw["w_up"], fw["conv"], fw["w_down"])


def _pool_consts(tm, row_len):
    pos = np.arange(tm)
    col = pos % row_len
    same_row = (pos[:, None] // row_len) == (pos[None, :] // row_len)
    band = np.zeros((len(POOL_WINDOWS), tm, tm), np.float32)
    icnt = np.zeros((tm, D_POOL), np.float32)
    for gi, win in enumerate(POOL_WINDOWS):
        lo = np.clip(col - win // 2, 0, row_len)
        hi = np.clip(col + win // 2, 0, row_len)
        inside = (col[None, :] >= lo[:, None]) & (col[None, :] < hi[:, None]) & same_row
        band[gi] = inside.astype(np.float32)
        icnt[:, gi * POOL_GROUP:(gi + 1) * POOL_GROUP] = (1.0 / (hi - lo).astype(np.float32))[:, None]
    return {"band": jnp.asarray(band, BF16), "icnt": jnp.asarray(icnt, F32)}


def _even_weights(i, ev_w_in, ev_w_out, ev_mu, ev_w0, ev_w_up, ev_a0, ev_a_up, ev_g_up, ev_k_k,
                  ev_k_a, ev_r_k, ev_gn_w, ev_gn_b, ev_pool_w, ev_pool_scale):
    o_lw = 3 * D_RWKV
    o_la = o_lw + 2 * D_DECAY_LORA
    o_lg = o_la + 2 * D_AAA_LORA
    o_pool = o_lg + D_GATE_LORA

    def repack(a, with_pool):
        z = lambda n: jnp.zeros(a.shape[:-1] + (n,), a.dtype)
        parts = [a[..., :o_lw], a[..., o_lw:o_la], z(LANES - 2 * D_DECAY_LORA),
                 a[..., o_la:o_lg], a[..., o_lg:o_pool], z(LANES - D_GATE_LORA)]
        if with_pool:
            parts.append(a[..., o_pool:])
        return jnp.concatenate(parts, axis=-1)

    def per_dir(w):
        rows = jnp.concatenate([jnp.pad(w[0], ((0, 0), (0, D_RWKV))), jnp.pad(w[1], ((0, 0), (D_RWKV, 0)))], axis=0)
        return jnp.pad(rows, ((0, LANES - rows.shape[0]), (0, 0)))

    wup = per_dir(ev_w_up[i])
    aup = per_dir(ev_a_up[i])
    gup = jnp.pad(ev_g_up[i], ((0, LANES - D_GATE_LORA), (0, 0)))
    row = lambda a: a.reshape(1, D_RWKV)
    vec = jnp.concatenate([ev_w0[i], ev_a0[i], row(ev_k_k[i]), row(ev_k_a[i]), row(ev_r_k[i]), row(ev_gn_w[i]),
                           row(ev_gn_b[i]), row(ev_pool_scale[i]), jnp.zeros((6, D_RWKV), F32)], axis=0)
    head = np.arange(MXU_DIM) // HEAD_DIM
    seg = jnp.asarray(head[:, None] == head[None, :], BF16)
    return {"w_in": repack(ev_w_in[i], True).astype(BF16), "mu": repack(ev_mu[i], False),
            "wup": wup.astype(BF16), "aup": aup.astype(BF16), "gup": gup.astype(BF16),
            "vec": vec, "seg": seg, "pool_w": ev_pool_w[i].astype(BF16),
            "w_out": ev_w_out[i].astype(BF16)}


def kernel(x, c, ctx, c_ctx, w_mod, b_mod, norm_g, ffn_w_up, ffn_conv, ffn_w_down, ev_w_in, ev_w_out,
           ev_mu, ev_w0, ev_w_up, ev_a0, ev_a_up, ev_g_up, ev_k_k, ev_k_a, ev_r_k, ev_gn_w, ev_gn_b,
           ev_pool_w, ev_pool_scale, od_w_in, od_conv, od_w_out):
    bsz, seq, _ = x.shape
    ctx_len = ctx.shape[1]
    assert SUB_MIX % POOL_TILE == 0 and POOL_TILE % GRID_W == 0
    assert ctx_len <= min(SUB_MIX, TM_EVEN_IN) and ctx_len % (CHUNK * CHUNKS_PER_STEP) == 0 and bsz < 16

    cv = jnp.concatenate([c, c_ctx[None], jnp.zeros((15 - bsz, D_MODEL), F32)], axis=0)
    mods = _mods(cv, w_mod, b_mod)
    pc_lat = _pool_consts(POOL_TILE, GRID_W)
    pc_ctx = _pool_consts(ctx_len, ctx_len)
    ffn_w_up_b = ffn_w_up.astype(BF16)
    ffn_w_down_b = ffn_w_down.astype(BF16)
    od_w_in_b = od_w_in.astype(BF16)
    od_w_out_b = od_w_out.astype(BF16)

    for layer in range(DEPTH):
        i = layer // 2
        even = layer % 2 == 0
        ctx_later = any(jj % 2 == 0 for jj in range(layer + 1, DEPTH))
        mod = mods[layer, :bsz].reshape(bsz, 6, D_MODEL)
        mod_c = mods[layer, bsz:bsz + 1].reshape(1, 6, D_MODEL)
        gn = norm_g[layer]
        fw = {"w_up": ffn_w_up_b[layer], "conv": ffn_conv[layer], "w_down": ffn_w_down_b[layer]}
        if even:
            ew = _even_weights(i, ev_w_in, ev_w_out, ev_mu, ev_w0, ev_w_up, ev_a0, ev_a_up, ev_g_up,
                               ev_k_k, ev_k_a, ev_r_k, ev_gn_w, ev_gn_b, ev_pool_w, ev_pool_scale)
            sh_c, dr_c, mg_c = _even_in(ctx, mod_c, gn, ew)
            sh_l, dr_l, mg_l = _even_in(x, mod, gn, ew)
            zero = jnp.zeros((bsz, 2, N_PAIRS, HEAD_DIM, LANES), F32)
            yf_c, yb_c, s_c = _wkv(sh_c, dr_c, zero)
            yf_l, yb_l, _ = _wkv(sh_l, dr_l, s_c)
            x = _even_out_ffn(x, yf_l, yb_l, mg_l, mod, gn, ew, pc_lat, fw, GRID_W)
            if ctx_later:
                ctx = _even_out_ffn(ctx, yf_c, yb_c, mg_c, mod_c, gn, ew, pc_ctx, fw, ctx_len)
        else:
            ow = {"w_in": od_w_in_b[i], "conv": od_conv[i], "w_out": od_w_out_b[i]}
            x = _odd_mix_ffn(x, mod, gn, ow, fw, GRID_W)
            if ctx_later:
                ctx = _odd_mix_ffn(ctx, mod_c, gn, ow, fw, ctx_len)
    return x
```

```python
import functools
import math

import numpy as np
import jax
import jax.numpy as jnp
from jax import lax
from jax.experimental import pallas as pl
from jax.experimental.pallas import tpu as pltpu

D_MODEL = 1024
DEPTH = 4
GRID_W = 64
HEAD_DIM = 64
N_HEADS = 8
D_RWKV = N_HEADS * HEAD_DIM
D_POOL = D_MODEL - D_RWKV
POOL_WINDOWS = (2, 4, 8, 16)
POOL_GROUP = D_POOL // len(POOL_WINDOWS)
D_DECAY_LORA = 32
D_AAA_LORA = 64
D_GATE_LORA = 96
D_FF = 2816
RMS_EPS = 1e-6
GN_EPS = 64e-5
DECAY_SCALE = math.exp(-0.5)

LANES = 128
SUBLANES = 8
MXU_DIM = 256
HEADS_PER_GROUP = 2
GW = HEADS_PER_GROUP * HEAD_DIM
N_GROUPS = N_HEADS // HEADS_PER_GROUP

C_LW = 3 * D_RWKV
C_LA = C_LW + LANES
C_LG = C_LA + LANES
C_SHIFT = C_LG + LANES
C_POOL = C_SHIFT
C_EVEN = C_POOL + D_POOL

HALO = SUBLANES
CHUNK = 64
CHUNKS_PER_STEP = 8
TM_EVEN_IN = 256
SUB_EVEN_IN = 256
TM_MIX = 512
SUB_MIX = 256
POOL_TILE = 256
FF_SLABS = ((0, 1536), (1536, D_FF))

F32 = jnp.float32
BF16 = jnp.bfloat16
HIGHEST = lax.Precision.HIGHEST
VMEM_LIMIT = 56 * 1024 * 1024


def _sigmoid(x):
    return 1.0 / (1.0 + jnp.exp(-x))


def _rms(x, g):
    ms = jnp.mean(x * x, axis=-1, keepdims=True)
    return x * lax.rsqrt(ms + RMS_EPS) * g


def _norm_mod(x, g, shift, scale):
    return _rms(x, g) * (1.0 + scale) + shift


def _bdot(a, b):
    return jnp.dot(a.astype(BF16), b.astype(BF16), preferred_element_type=F32)


def _head_sums(x, e):
    m, width = x.shape
    w = e.shape[0]
    xs = jnp.concatenate([x[:, i * w:(i + 1) * w] for i in range(width // w)], axis=0)
    hi = xs.astype(BF16)
    lo = (xs - hi.astype(F32)).astype(BF16)
    s = jnp.dot(hi, e, preferred_element_type=F32) + jnp.dot(lo, e, preferred_element_type=F32)
    return jnp.concatenate([s[i * m:(i + 1) * m] for i in range(width // w)], axis=1)


def _split_dot_lhs01(e, x):
    hi = x.astype(BF16)
    lo = (x - hi.astype(F32)).astype(BF16)
    return (jnp.dot(e, hi, preferred_element_type=F32)
            + jnp.dot(e, lo, preferred_element_type=F32))


def _row_edges(tm, row_len):
    pos = lax.broadcasted_iota(jnp.int32, (tm, 1), 0) & (row_len - 1)
    return pos == 0, pos == row_len - 1


def _dwconv3(x, w, first, last, tm):
    prev = jnp.where(first, 0.0, pltpu.roll(x, 1, 0))
    nxt = jnp.where(last, 0.0, pltpu.roll(x, tm - 1, 0))
    return prev * w[0:1, :] + x * w[1:2, :] + nxt * w[2:3, :]


def _const_spec(shape):
    nd = len(shape)
    return pl.BlockSpec(shape, lambda *_: (0,) * nd, pipeline_mode=pl.Buffered(1))


def _layer_spec(shape, layer):
    nd = len(shape)
    return pl.BlockSpec((1,) + shape, lambda *_: (layer,) + (0,) * nd, pipeline_mode=pl.Buffered(1))


def _params(n_axes):
    return pltpu.CompilerParams(dimension_semantics=("arbitrary",) * n_axes,
                                vmem_limit_bytes=VMEM_LIMIT)


def _mods_kernel(cv_ref, w_ref, b_ref, o_ref):
    cv = cv_ref[...]
    s = cv * _sigmoid(cv)
    o_ref[0] = jnp.dot(s, w_ref[0], preferred_element_type=F32, precision=HIGHEST) + b_ref[0]


def _mods(cv, w_mod, b_mod):
    tn = 1536
    n6 = 6 * D_MODEL
    return pl.pallas_call(
        _mods_kernel,
        grid=(DEPTH, n6 // tn),
        in_specs=[pl.BlockSpec((16, D_MODEL), lambda l, n: (0, 0)),
                  pl.BlockSpec((1, D_MODEL, tn), lambda l, n: (l, 0, n)),
                  pl.BlockSpec((1, 1, tn), lambda l, n: (l, 0, n))],
        out_specs=pl.BlockSpec((1, 16, tn), lambda l, n: (l, 0, n)),
        out_shape=jax.ShapeDtypeStruct((DEPTH, 16, n6), F32),
        compiler_params=_params(2),
        name="mods",
    )(cv, w_mod, b_mod.reshape(DEPTH, 1, n6))


def _even_in_chain(s, n_sub, tm, j, nt, x_ref, xp_ref, xn_ref, mod_ref, gn_ref, win_ref, mu_ref, wup_ref,
                   aup_ref, gup_ref, vec_ref, e_ref, sh_ref, dr_ref, mg_ref):
    te = tm + 2 * HALO
    rows = slice(s * tm, (s + 1) * tm)
    top = xp_ref[0] if s == 0 else x_ref[0, s * tm - HALO:s * tm, :]
    bot = xn_ref[0] if s == n_sub - 1 else x_ref[0, (s + 1) * tm:(s + 1) * tm + HALO, :]
    xe = jnp.concatenate([top, x_ref[0, rows, :], bot], axis=0)
    h32 = _norm_mod(xe, gn_ref[0:1, :], mod_ref[0, 0:1, :], mod_ref[0, 1:2, :])
    h = h32.astype(BF16)

    def proj(c0, c1):
        return jnp.dot(h, win_ref[:, c0:c1], preferred_element_type=F32)

    def shifted(pb, c0):
        c1 = c0 + pb.shape[1]
        pc = pb[HALO:tm + HALO]
        p_top = jnp.where(j == 0, 0.0, pb[:HALO]) if s == 0 else pb[:HALO]
        p_bot = jnp.where(j == nt - 1, 0.0, pb[tm + HALO:]) if s == n_sub - 1 else pb[tm + HALO:]
        pe = jnp.concatenate([p_top, pc, p_bot], axis=0)
        prev = pltpu.roll(pe, 1, 0)[HALO:tm + HALO]
        nxt = pltpu.roll(pe, te - 1, 0)[HALO:tm + HALO]
        mu0 = mu_ref[0:1, c0:c1]
        mu1 = mu_ref[1:2, c0:c1]
        return pc * (1.0 - mu0 - mu1) + mu0 * prev + mu1 * nxt

    p_k = proj(D_RWKV, 2 * D_RWKV)
    p_l = proj(C_LW, C_SHIFT)
    p_r = proj(0, D_RWKV)
    p_v = proj(2 * D_RWKV, 3 * D_RWKV)
    p_pool = jnp.dot(h32[HALO:tm + HALO].astype(BF16), win_ref[:, C_POOL:C_EVEN], preferred_element_type=F32)
    yield
    k = shifted(p_k, D_RWKV)
    lora = shifted(p_l, C_LW)
    wd = _bdot(jnp.tanh(lora[:, 0:LANES]), wup_ref[...])
    la = _bdot(lora[:, C_LA - C_LW:C_LA - C_LW + LANES], aup_ref[...])
    g = _bdot(_sigmoid(lora[:, C_LG - C_LW:C_LG - C_LW + LANES]), gup_ref[...])
    r = shifted(p_r, 0)
    v = shifted(p_v, 2 * D_RWKV)
    k_k = vec_ref[4:5, :]
    k_a = vec_ref[5:6, :]
    r_k = vec_ref[6:7, :]
    kkr = k * k_k
    nrm = _head_sums(kkr * kkr, e_ref[...])
    yield
    kk = kkr * lax.rsqrt(jnp.maximum(nrm, 1e-24))
    kd_sum = None
    for d in range(2):
        wdd = vec_ref[d:d + 1, :] + wd[:, d * D_RWKV:(d + 1) * D_RWKV]
        ld = -DECAY_SCALE * _sigmoid(wdd)
        a = _sigmoid(vec_ref[2 + d:3 + d, :] + la[:, d * D_RWKV:(d + 1) * D_RWKV])
        kd = k * (1.0 + (a - 1.0) * k_a)
        dr_ref[d, 0, rows, 0:D_RWKV] = ld
        dr_ref[d, 0, rows, D_RWKV:2 * D_RWKV] = kd
        dr_ref[d, 0, rows, 2 * D_RWKV:3 * D_RWKV] = kk * a
        kd_sum = kd if kd_sum is None else kd_sum + kd
    coef = _head_sums(r * kd_sum * r_k, e_ref[...])
    yield
    sh_ref[0, rows, 0:D_RWKV] = r
    sh_ref[0, rows, D_RWKV:2 * D_RWKV] = v
    sh_ref[0, rows, 2 * D_RWKV:3 * D_RWKV] = kk
    mg_ref[0, rows, 0:D_RWKV] = coef * v
    mg_ref[0, rows, D_RWKV:2 * D_RWKV] = g
    mg_ref[0, rows, 2 * D_RWKV:3 * D_RWKV] = p_pool


def _even_in_kernel(*refs, sub, n_sub, nt):
    j = pl.program_id(1)
    _round_robin([_even_in_chain(s, n_sub, sub, j, nt, *refs) for s in range(n_sub)])


def _even_in(x, mod, gn, ew):
    bsz, seq, _ = x.shape
    tm = min(TM_EVEN_IN, seq)
    sub = min(SUB_EVEN_IN, tm)
    assert seq % tm == 0 and tm % sub == 0
    nt = seq // tm
    hb = tm // HALO
    nhb = seq // HALO
    c3 = 3 * D_RWKV
    kern = functools.partial(_even_in_kernel, sub=sub, n_sub=tm // sub, nt=nt)
    mod_map = (lambda b, j: (b, 0, 0)) if mod.shape[0] == bsz else (lambda b, j: (0, 0, 0))
    return pl.pallas_call(
        kern,
        grid=(bsz, nt),
        in_specs=[pl.BlockSpec((1, tm, D_MODEL), lambda b, j: (b, j, 0)),
                  pl.BlockSpec((1, HALO, D_MODEL), lambda b, j: (b, jnp.maximum(j * hb - 1, 0), 0)),
                  pl.BlockSpec((1, HALO, D_MODEL), lambda b, j: (b, jnp.minimum((j + 1) * hb, nhb - 1), 0)),
                  pl.BlockSpec((1, 6, D_MODEL), mod_map),
                  _const_spec((4, D_MODEL)),
                  _const_spec((D_MODEL, C_EVEN)),
                  _const_spec((2, C_SHIFT)),
                  _const_spec((LANES, 2 * D_RWKV)),
                  _const_spec((LANES, 2 * D_RWKV)),
                  _const_spec((LANES, D_RWKV)),
                  _const_spec((16, D_RWKV)),
                  _const_spec((MXU_DIM, MXU_DIM))],
        out_specs=[pl.BlockSpec((1, tm, c3), lambda b, j: (b, j, 0)),
                   pl.BlockSpec((2, 1, tm, c3), lambda b, j: (0, b, j, 0)),
                   pl.BlockSpec((1, tm, c3), lambda b, j: (b, j, 0))],
        out_shape=[jax.ShapeDtypeStruct((bsz, seq, c3), F32),
                   jax.ShapeDtypeStruct((2, bsz, seq, c3), F32),
                   jax.ShapeDtypeStruct((bsz, seq, c3), F32)],
        compiler_params=_params(2),
        name="even_in",
    )(x, x, x, mod, gn, ew["w_in"], ew["mu"], ew["wup"], ew["aup"], ew["gup"], ew["vec"], ew["seg"])


def _mm(a, b):
    return jnp.dot(a.astype(BF16), b.astype(BF16), preferred_element_type=F32)


def _mm_nt(a, b):
    return lax.dot_general(a.astype(BF16), b.astype(BF16), (((1,), (1,)), ((), ())),
                           preferred_element_type=F32)


def _mm_tn(a, b):
    return lax.dot_general(a.astype(BF16), b.astype(BF16), (((0,), (0,)), ((), ())),
                           preferred_element_type=F32)


def _cumsum_rows(x, reverse, n):
    rows = lax.broadcasted_iota(jnp.int32, (n, 1), 0)
    s = 1
    while s < n:
        if s < SUBLANES:
            if reverse:
                shifted = jnp.where(rows < n - s, pltpu.roll(x, n - s, 0), 0.0)
            else:
                shifted = jnp.where(rows >= s, pltpu.roll(x, s, 0), 0.0)
        else:
            zeros = jnp.zeros((s, x.shape[1]), x.dtype)
            shifted = (jnp.concatenate([x[s:], zeros], axis=0) if reverse
                       else jnp.concatenate([zeros, x[:n - s]], axis=0))
        x = x + shifted
        s *= 2
    return x


def _wkv_masks(reverse, L):
    row = lax.broadcasted_iota(jnp.int32, (L, GW), 0)
    lane = lax.broadcasted_iota(jnp.int32, (L, GW), 1)
    colp = lane & (HEAD_DIM - 1)
    row2 = lax.broadcasted_iota(jnp.int32, (HEADS_PER_GROUP * L, GW), 0)
    lane2 = lax.broadcasted_iota(jnp.int32, (HEADS_PER_GROUP * L, GW), 1)
    blk = row ^ colp
    m = {
        "reverse": reverse,
        "head": lane // HEAD_DIM,
        "incl": (row <= colp) if reverse else (row >= colp),
        "strict": (row < colp) if reverse else (row > colp),
        "eye": (row == colp).astype(F32),
        "diag_blocks": (row2 // L) == (lane2 // HEAD_DIM),
        "same2": (blk >> 1) == 0,
    }
    g, lg = 2, 1
    while g < L:
        m["off%d" % g] = (blk >> lg) == 1
        g, lg = 2 * g, lg + 1
    return m


def _bd(x, head):
    return jnp.concatenate([jnp.where(head == h, x, 0.0) for h in range(HEADS_PER_GROUP)], axis=0)


def _pd(y, head, L):
    out = y[(HEADS_PER_GROUP - 1) * L:]
    for h in range(HEADS_PER_GROUP - 2, -1, -1):
        out = jnp.where(head == h, y[h * L:(h + 1) * L], out)
    return out


def _wkv_precompute(sh, dr, m, L):
    head = m["head"]
    reverse = m["reverse"]
    bd = lambda x: _bd(x, head)

    r = sh[:, 0:D_RWKV]
    v = sh[:, D_RWKV:2 * D_RWKV]
    kk = sh[:, 2 * D_RWKV:3 * D_RWKV]
    ld = dr[:, 0:D_RWKV]
    kd = dr[:, D_RWKV:2 * D_RWKV]
    b = dr[:, 2 * D_RWKV:3 * D_RWKV]

    cum = _cumsum_rows(ld, reverse, L)
    tot = cum[0:1, :] if reverse else cum[L - 1:L, :]
    kkt = kk * jnp.exp(cum - ld)
    rt = r * jnp.exp(cum)
    iw = jnp.exp(-cum)
    kh = kd * iw
    bh = b * iw
    wl = jnp.exp(tot - cum)
    khw = kd * wl
    bhw = b * wl
    wtot = jnp.exp(tot)

    def group_chain(p, res):
        sl = slice(p * GW, (p + 1) * GW)
        kkt_p, rt_p, kh_p, bh_p = kkt[:, sl], rt[:, sl], kh[:, sl], bh[:, sl]
        v_p, khw_p, bhw_p = v[:, sl], khw[:, sl], bhw[:, sl]
        lhs = jnp.concatenate([kkt_p, rt_p], axis=0)
        sc = _mm_nt(lhs, jnp.concatenate([bd(bh_p), bd(kh_p)], axis=0))
        yield
        sb = sc[:, :GW]
        sk = sc[:, GW:]
        a_ab = jnp.where(m["strict"], sb[:L], 0.0)
        a_rb = jnp.where(m["incl"], sb[L:], 0.0)
        a_ak = jnp.where(m["strict"], sk[:L], 0.0)
        a_rk = jnp.where(m["incl"], sk[L:], 0.0)
        avk = _mm(jnp.concatenate([a_ak, a_rk], axis=0), bd(v_p))
        t = m["eye"] - jnp.where(m["same2"], a_ab, 0.0)
        g = 2
        while g < L:
            w = _mm(t, bd(jnp.where(m["off%d" % g], a_ab, 0.0)))
            yield
            t = t - _mm(w, bd(t))
            yield
            g *= 2
        pu = _mm(t, jnp.concatenate([bd(kkt_p), bd(avk[:L])], axis=1))
        yield
        pm = pu[:, :GW]
        u0 = pu[:, GW:]
        qy = _mm(a_rb, jnp.concatenate([bd(pm), bd(u0)], axis=1))
        mc = jnp.where(m["diag_blocks"], _mm_tn(pm, bhw_p), 0.0)
        gg = _pd(_mm_tn(jnp.concatenate([v_p, u0], axis=0),
                        jnp.concatenate([khw_p, -bhw_p], axis=0)), head, L)
        q = rt_p - qy[:, :GW]
        y0 = avk[L:] - qy[:, GW:]
        res[p] = (q, y0, mc, gg, wtot[:, sl])

    res = [None] * N_GROUPS
    return res, [group_chain(p, res) for p in range(N_GROUPS)]


def _round_robin(chains):
    chains = list(chains)
    while chains:
        alive = []
        for ch in chains:
            try:
                next(ch)
                alive.append(ch)
            except StopIteration:
                pass
        chains = alive


def _wkv_kernel(shf_ref, shb_ref, drf_ref, drb_ref, s0_ref, yf_ref, yb_ref, sout_ref, s_ref,
                *, L, cps, nsteps):
    j = pl.program_id(1)

    @pl.when(j == 0)
    def _():
        s_ref[...] = s0_ref[0]

    io = ((shf_ref, drf_ref, yf_ref), (shb_ref, drb_ref, yb_ref))
    order = (list(range(cps)), list(range(cps - 1, -1, -1)))
    results, chains = {}, []
    for d, (sh_ref, dr_ref, _) in enumerate(io):
        m = _wkv_masks(d == 1, L)
        for c in order[d]:
            rows = slice(c * L, (c + 1) * L)
            results[d, c], ch = _wkv_precompute(sh_ref[0, rows, :], dr_ref[0, 0, rows, :], m, L)
            chains += ch
    _round_robin(chains)

    head = lax.broadcasted_iota(jnp.int32, (L, GW), 1) // HEAD_DIM
    state = [[s_ref[d, p] for p in range(N_GROUPS)] for d in range(2)]
    for i in range(cps):
        old = [[None] * N_GROUPS for _ in range(2)]
        for d in range(2):
            for p in range(N_GROUPS):
                _, _, mc, gg, wtot = results[d, order[d][i]][p]
                s0 = state[d][p]
                old[d][p] = s0
                state[d][p] = s0 * wtot - _mm(s0, mc) + gg
        for d in range(2):
            c = order[d][i]
            for p in range(N_GROUPS):
                q, y0 = results[d, c][p][:2]
                io[d][2][0, c * L:(c + 1) * L, p * GW:(p + 1) * GW] = _mm_nt(q, _bd(old[d][p], head)) + y0
    for d in range(2):
        for p in range(N_GROUPS):
            s_ref[d, p] = state[d][p]

    @pl.when(j == nsteps - 1)
    def _():
        sout_ref[0] = s_ref[...]


def _wkv(sh, dr, s0):
    bsz, seq, c3 = sh.shape
    L = CHUNK
    assert L == HEAD_DIM
    cps = min(CHUNKS_PER_STEP, seq // L)
    tb = cps * L
    assert seq % tb == 0
    ns = seq // tb
    kern = functools.partial(_wkv_kernel, L=L, cps=cps, nsteps=ns)
    st_shape = (1, 2, N_GROUPS, HEAD_DIM, GW)
    return pl.pallas_call(
        kern,
        grid=(bsz, ns),
        in_specs=[pl.BlockSpec((1, tb, c3), lambda b, j: (b, j, 0)),
                  pl.BlockSpec((1, tb, c3), lambda b, j: (b, ns - 1 - j, 0)),
                  pl.BlockSpec((1, 1, tb, c3), lambda b, j: (0, b, j, 0)),
                  pl.BlockSpec((1, 1, tb, c3), lambda b, j: (1, b, ns - 1 - j, 0)),
                  pl.BlockSpec(st_shape, lambda b, j: (b, 0, 0, 0, 0))],
        out_specs=[pl.BlockSpec((1, tb, D_RWKV), lambda b, j: (b, j, 0)),
                   pl.BlockSpec((1, tb, D_RWKV), lambda b, j: (b, ns - 1 - j, 0)),
                   pl.BlockSpec(st_shape, lambda b, j: (b, 0, 0, 0, 0))],
        out_shape=[jax.ShapeDtypeStruct((bsz, seq, D_RWKV), F32),
                   jax.ShapeDtypeStruct((bsz, seq, D_RWKV), F32),
                   jax.ShapeDtypeStruct((bsz, 2, N_GROUPS, HEAD_DIM, GW), F32)],
        scratch_shapes=[pltpu.VMEM((2, N_GROUPS, HEAD_DIM, GW), F32)],
        compiler_params=_params(2),
        name="wkv",
    )(sh, sh, dr, dr, s0)


def _ffn_chain(x, mod_ref, gn_ref, wup_ref, cw_ref, wdn_ref, tm, row_len, store):
    h = _norm_mod(x, gn_ref[2:3, :], mod_ref[0, 3:4, :], mod_ref[0, 4:5, :]).astype(BF16)
    first, last = _row_edges(tm, row_len)
    yo = None
    for c0, c1 in FF_SLABS:
        a = jnp.dot(h, wup_ref[0, :, c0:c1], preferred_element_type=F32)
        gate = jnp.dot(h, wup_ref[0, :, D_FF + c0:D_FF + c1], preferred_element_type=F32)
        yield
        c = _dwconv3(a, cw_ref[0, :, c0:c1], first, last, tm)
        u = (c * _sigmoid(c)) * gate
        part = jnp.dot(u.astype(BF16), wdn_ref[0, c0:c1, :], preferred_element_type=F32)
        yield
        yo = part if yo is None else yo + part
    store(x + mod_ref[0, 5:6, :] * _rms(yo, gn_ref[3:4, :]))


def _ffn_specs(layer):
    return [_layer_spec((D_MODEL, 2 * D_FF), layer), _layer_spec((3, D_FF), layer),
            _layer_spec((D_FF, D_MODEL), layer)]


def _token_tiling(x, mod, row_len):
    bsz, seq, _ = x.shape
    tm = min(TM_MIX, seq)
    sub = min(SUB_MIX, tm)
    assert seq % tm == 0 and tm % sub == 0 and sub % row_len == 0 and row_len & (row_len - 1) == 0
    assert all(c % MXU_DIM == 0 for c, _ in FF_SLABS) and FF_SLABS[-1][1] == D_FF and D_RWKV % MXU_DIM == 0
    mod_map = (lambda b, j: (b, 0, 0)) if mod.shape[0] == bsz else (lambda b, j: (0, 0, 0))
    return bsz, seq // tm, tm, sub, pl.BlockSpec((1, 6, D_MODEL), mod_map)


def _even_out_chain(rows, x_ref, yf_ref, yb_ref, mg_ref, mod_ref, gn_ref, vec_ref, e_ref, band_ref,
                    icnt_ref, pw_ref, wout_ref, wup_ref, cw_ref, wdn_ref, o_ref, tm, pt, row_len):
    y = yf_ref[0, rows, :] + yb_ref[0, rows, :]
    inv_n = 1.0 / HEAD_DIM
    mean = _head_sums(y, e_ref[...]) * inv_n
    yield
    yc = y - mean
    var = _head_sums(yc * yc, e_ref[...]) * inv_n
    yield
    yn = yc * lax.rsqrt(var + GN_EPS) * vec_ref[7:8, :] + vec_ref[8:9, :]
    y_rwkv = (yn + mg_ref[0, rows, 0:D_RWKV]) * mg_ref[0, rows, D_RWKV:2 * D_RWKV]

    pp = mg_ref[0, rows, 2 * D_RWKV:3 * D_RWKV]
    outs = []
    for gi in range(len(POOL_WINDOWS)):
        sl = slice(gi * POOL_GROUP, (gi + 1) * POOL_GROUP)
        xg = pp[:, sl]
        wsum = jnp.concatenate([_split_dot_lhs01(band_ref[gi], xg[s * pt:(s + 1) * pt])
                                for s in range(tm // pt)], axis=0)
        icnt = jnp.concatenate([icnt_ref[:, sl]] * (tm // pt), axis=0)
        dg = wsum * icnt - xg
        outs.append(_bdot(dg, pw_ref[gi]))
    yield
    y_pool = jnp.concatenate(outs, axis=-1) * vec_ref[9:10, :]

    cat = jnp.concatenate([y_rwkv, y_pool], axis=-1)
    yo = _bdot(cat, wout_ref[...])
    yield
    x1 = x_ref[0, rows, :] + mod_ref[0, 2:3, :] * _rms(yo, gn_ref[1:2, :])

    def store(val):
        o_ref[0, rows, :] = val

    yield from _ffn_chain(x1, mod_ref, gn_ref, wup_ref, cw_ref, wdn_ref, tm, row_len, store)


def _even_out_kernel(*refs, tm, sub, pt, row_len):
    _round_robin([_even_out_chain(slice(s * sub, (s + 1) * sub), *refs, sub, pt, row_len)
                  for s in range(tm // sub)])


def _even_out_ffn(x, yf, yb, mg, mod, gn, ew, pc, fw, row_len):
    bsz, nt, tm, sub, mod_spec = _token_tiling(x, mod, row_len)
    c3 = 3 * D_RWKV
    tok = lambda w: pl.BlockSpec((1, tm, w), lambda b, j: (b, j, 0))
    pt = pc["band"].shape[1]
    assert sub % pt == 0
    return pl.pallas_call(
        functools.partial(_even_out_kernel, tm=tm, sub=sub, pt=pt, row_len=row_len),
        grid=(bsz, nt),
        in_specs=[tok(D_MODEL), tok(D_RWKV), tok(D_RWKV), tok(c3), mod_spec,
                  _const_spec((4, D_MODEL)),
                  _const_spec((16, D_RWKV)),
                  _const_spec((MXU_DIM, MXU_DIM)),
                  _const_spec((4, pt, pt)),
                  _const_spec((pt, D_POOL)),
                  _const_spec((4, POOL_GROUP, POOL_GROUP)),
                  _const_spec((D_MODEL, D_MODEL))] + _ffn_specs(fw["layer"]),
        out_specs=tok(D_MODEL),
        out_shape=jax.ShapeDtypeStruct(x.shape, F32),
        compiler_params=_params(2),
        name="even_out_ffn",
    )(x, yf, yb, mg, mod, gn, ew["vec"], ew["seg"], pc["band"], pc["icnt"], ew["pool_w"], ew["w_out"],
      fw["w_up"], fw["conv"], fw["w_down"])


def _odd_chain(rows, x_ref, mod_ref, gn_ref, win_ref, cw_ref, wout_ref, fup_ref, fcw_ref, fdn_ref, o_ref,
               tm, row_len):
    x = x_ref[0, rows, :]
    h = _norm_mod(x, gn_ref[0:1, :], mod_ref[0, 0:1, :], mod_ref[0, 1:2, :])
    p = jnp.dot(h.astype(BF16), win_ref[0], preferred_element_type=F32)
    yield
    bg = p[:, 0:D_MODEL]
    cg = p[:, D_MODEL:2 * D_MODEL]
    u = p[:, 2 * D_MODEL:3 * D_MODEL]
    first, last = _row_edges(tm, row_len)
    z = bg * _dwconv3(cg * u, cw_ref[0], first, last, tm)
    yo = _bdot(z, wout_ref[0])
    yield
    x1 = x + mod_ref[0, 2:3, :] * _rms(yo, gn_ref[1:2, :])

    def store(val):
        o_ref[0, rows, :] = val

    yield from _ffn_chain(x1, mod_ref, gn_ref, fup_ref, fcw_ref, fdn_ref, tm, row_len, store)


def _odd_kernel(*refs, tm, sub, row_len):
    _round_robin([_odd_chain(slice(s * sub, (s + 1) * sub), *refs, sub, row_len) for s in range(tm // sub)])


def _odd_mix_ffn(x, mod, gn, ow, fw, row_len):
    bsz, nt, tm, sub, mod_spec = _token_tiling(x, mod, row_len)
    tok = pl.BlockSpec((1, tm, D_MODEL), lambda b, j: (b, j, 0))
    return pl.pallas_call(
        functools.partial(_odd_kernel, tm=tm, sub=sub, row_len=row_len),
        grid=(bsz, nt),
        in_specs=[tok, mod_spec,
                  _const_spec((4, D_MODEL)),
                  _layer_spec((D_MODEL, 3 * D_MODEL), ow["layer"]),
                  _layer_spec((3, D_MODEL), ow["layer"]),
                  _layer_spec((D_MODEL, D_MODEL), ow["layer"])] + _ffn_specs(fw["layer"]),
        out_specs=tok,
        out_shape=jax.ShapeDtypeStruct(x.shape, F32),
        compiler_params=_params(2),
        name="odd_mix_ffn",
    )(x, mod, gn, ow["w_in"], ow["conv"], ow["w_out"], fw["w_up"], fw["conv"], fw["w_down"])


def _pool_consts(tm, row_len):
    pos = np.arange(tm)
    col = pos % row_len
    same_row = (pos[:, None] // row_len) == (pos[None, :] // row_len)
    band = np.zeros((len(POOL_WINDOWS), tm, tm), np.float32)
    icnt = np.zeros((tm, D_POOL), np.float32)
    for gi, win in enumerate(POOL_WINDOWS):
        lo = np.clip(col - win // 2, 0, row_len)
        hi = np.clip(col + win // 2, 0, row_len)
        inside = (col[None, :] >= lo[:, None]) & (col[None, :] < hi[:, None]) & same_row
        band[gi] = inside.astype(np.float32)
        icnt[:, gi * POOL_GROUP:(gi + 1) * POOL_GROUP] = (1.0 / (hi - lo).astype(np.float32))[:, None]
    return {"band": jnp.asarray(band, BF16), "icnt": jnp.asarray(icnt, F32)}


def _even_weights(i, ev_w_in, ev_w_out, ev_mu, ev_w0, ev_w_up, ev_a0, ev_a_up, ev_g_up, ev_k_k,
                  ev_k_a, ev_r_k, ev_gn_w, ev_gn_b, ev_pool_w, ev_pool_scale):
    o_lw = 3 * D_RWKV
    o_la = o_lw + 2 * D_DECAY_LORA
    o_lg = o_la + 2 * D_AAA_LORA
    o_pool = o_lg + D_GATE_LORA

    def repack(a, with_pool):
        z = lambda n: jnp.zeros(a.shape[:-1] + (n,), a.dtype)
        parts = [a[..., :o_lw], a[..., o_lw:o_la], z(LANES - 2 * D_DECAY_LORA),
                 a[..., o_la:o_lg], a[..., o_lg:o_pool], z(LANES - D_GATE_LORA)]
        if with_pool:
            parts.append(a[..., o_pool:])
        return jnp.concatenate(parts, axis=-1)

    def per_dir(w):
        rows = jnp.concatenate([jnp.pad(w[0], ((0, 0), (0, D_RWKV))), jnp.pad(w[1], ((0, 0), (D_RWKV, 0)))], axis=0)
        return jnp.pad(rows, ((0, LANES - rows.shape[0]), (0, 0)))

    wup = per_dir(ev_w_up[i])
    aup = per_dir(ev_a_up[i])
    gup = jnp.pad(ev_g_up[i], ((0, LANES - D_GATE_LORA), (0, 0)))
    row = lambda a: a.reshape(1, D_RWKV)
    vec = jnp.concatenate([ev_w0[i], ev_a0[i], row(ev_k_k[i]), row(ev_k_a[i]), row(ev_r_k[i]), row(ev_gn_w[i]),
                           row(ev_gn_b[i]), row(ev_pool_scale[i]), jnp.zeros((6, D_RWKV), F32)], axis=0)
    head = np.arange(MXU_DIM) // HEAD_DIM
    seg = jnp.asarray(head[:, None] == head[None, :], BF16)
    return {"w_in": repack(ev_w_in[i], True).astype(BF16), "mu": repack(ev_mu[i], False),
            "wup": wup.astype(BF16), "aup": aup.astype(BF16), "gup": gup.astype(BF16),
            "vec": vec, "seg": seg, "pool_w": ev_pool_w[i].astype(BF16),
            "w_out": ev_w_out[i].astype(BF16)}


def kernel(x, c, ctx, c_ctx, w_mod, b_mod, norm_g, ffn_w_up, ffn_conv, ffn_w_down, ev_w_in, ev_w_out,
           ev_mu, ev_w0, ev_w_up, ev_a0, ev_a_up, ev_g_up, ev_k_k, ev_k_a, ev_r_k, ev_gn_w, ev_gn_b,
           ev_pool_w, ev_pool_scale, od_w_in, od_conv, od_w_out):
    bsz, seq, _ = x.shape
    ctx_len = ctx.shape[1]
    assert SUB_MIX % POOL_TILE == 0 and POOL_TILE % GRID_W == 0
    assert ctx_len <= min(SUB_MIX, TM_EVEN_IN) and ctx_len % CHUNK == 0 and seq % (CHUNK * CHUNKS_PER_STEP) == 0 and bsz < 16

    cv = jnp.concatenate([c, c_ctx[None], jnp.zeros((15 - bsz, D_MODEL), F32)], axis=0)
    mods = _mods(cv, w_mod, b_mod)
    pc_lat = _pool_consts(POOL_TILE, GRID_W)
    pc_ctx = _pool_consts(ctx_len, ctx_len)
    ffn_w_up_b = ffn_w_up.astype(BF16)
    ffn_w_down_b = ffn_w_down.astype(BF16)
    od_w_in_b = od_w_in.astype(BF16)
    od_w_out_b = od_w_out.astype(BF16)

    for layer in range(DEPTH):
        i = layer // 2
        even = layer % 2 == 0
        ctx_later = any(jj % 2 == 0 for jj in range(layer + 1, DEPTH))
        mod = mods[layer, :bsz].reshape(bsz, 6, D_MODEL)
        mod_c = mods[layer, bsz:bsz + 1].reshape(1, 6, D_MODEL)
        gn = norm_g[layer]
        fw = {"w_up": ffn_w_up_b, "conv": ffn_conv, "w_down": ffn_w_down_b, "layer": layer}
        if even:
            ew = _even_weights(i, ev_w_in, ev_w_out, ev_mu, ev_w0, ev_w_up, ev_a0, ev_a_up, ev_g_up,
                               ev_k_k, ev_k_a, ev_r_k, ev_gn_w, ev_gn_b, ev_pool_w, ev_pool_scale)
            sh_c, dr_c, mg_c = _even_in(ctx, mod_c, gn, ew)
            sh_l, dr_l, mg_l = _even_in(x, mod, gn, ew)
            zero = jnp.zeros((bsz, 2, N_GROUPS, HEAD_DIM, GW), F32)
            yf_c, yb_c, s_c = _wkv(sh_c, dr_c, zero)
            yf_l, yb_l, _ = _wkv(sh_l, dr_l, s_c)
            x = _even_out_ffn(x, yf_l, yb_l, mg_l, mod, gn, ew, pc_lat, fw, GRID_W)
            if ctx_later:
                ctx = _even_out_ffn(ctx, yf_c, yb_c, mg_c, mod_c, gn, ew, pc_ctx, fw, ctx_len)
        else:
            ow = {"w_in": od_w_in_b, "conv": od_conv, "w_out": od_w_out_b, "layer": i}
            x = _odd_mix_ffn(x, mod, gn, ow, fw, GRID_W)
            if ctx_later:
                ctx = _odd_mix_ffn(ctx, mod_c, gn, ow, fw, ctx_len)
    return x
```

```python
import functools
import math

import numpy as np
import jax
import jax.numpy as jnp
from jax import lax
from jax.experimental import pallas as pl
from jax.experimental.pallas import tpu as pltpu

D_MODEL = 1024
DEPTH = 4
GRID_W = 64
HEAD_DIM = 64
N_HEADS = 8
D_RWKV = N_HEADS * HEAD_DIM
D_POOL = D_MODEL - D_RWKV
POOL_WINDOWS = (2, 4, 8, 16)
POOL_GROUP = D_POOL // len(POOL_WINDOWS)
D_DECAY_LORA = 32
D_AAA_LORA = 64
D_GATE_LORA = 96
D_FF = 2816
RMS_EPS = 1e-6
GN_EPS = 64e-5
DECAY_SCALE = math.exp(-0.5)

LANES = 128
SUBLANES = 8
MXU_DIM = 256
HEADS_PER_GROUP = 2
GW = HEADS_PER_GROUP * HEAD_DIM
N_GROUPS = N_HEADS // HEADS_PER_GROUP

C_LW = 3 * D_RWKV
C_LA = C_LW + LANES
C_LG = C_LA + LANES
C_SHIFT = C_LG + LANES
C_POOL = C_SHIFT
C_EVEN = C_POOL + D_POOL

HALO = SUBLANES
CHUNK = 64
CHUNKS_PER_STEP = 8
TM_EVEN_IN = 512
SUB_EVEN_IN = 512
TM_MIX = 512
SUB_MIX = 256
POOL_TILE = 256
FF_SLABS = ((0, D_FF),)

F32 = jnp.float32
BF16 = jnp.bfloat16
HIGHEST = lax.Precision.HIGHEST
VMEM_LIMIT = 56 * 1024 * 1024


def _sigmoid(x):
    return 1.0 / (1.0 + jnp.exp(-x))


def _rms(x, g):
    ms = jnp.mean(x * x, axis=-1, keepdims=True)
    return x * lax.rsqrt(ms + RMS_EPS) * g


def _norm_mod(x, g, shift, scale):
    return _rms(x, g) * (1.0 + scale) + shift


def _bdot(a, b):
    return jnp.dot(a.astype(BF16), b.astype(BF16), preferred_element_type=F32)


def _head_sums(x, e):
    m, width = x.shape
    w = e.shape[0]
    xs = jnp.concatenate([x[:, i * w:(i + 1) * w] for i in range(width // w)], axis=0)
    s = jnp.dot(xs.astype(BF16), e, preferred_element_type=F32)
    return jnp.concatenate([s[i * m:(i + 1) * m] for i in range(width // w)], axis=1)


def _split_dot_lhs01(e, x):
    hi = x.astype(BF16)
    lo = (x - hi.astype(F32)).astype(BF16)
    return (jnp.dot(e, hi, preferred_element_type=F32)
            + jnp.dot(e, lo, preferred_element_type=F32))


def _row_edges(tm, row_len):
    pos = lax.broadcasted_iota(jnp.int32, (tm, 1), 0) & (row_len - 1)
    return pos == 0, pos == row_len - 1


def _dwconv3(x, w, first, last, tm):
    prev = jnp.where(first, 0.0, pltpu.roll(x, 1, 0))
    nxt = jnp.where(last, 0.0, pltpu.roll(x, tm - 1, 0))
    return prev * w[0:1, :] + x * w[1:2, :] + nxt * w[2:3, :]


def _const_spec(shape):
    nd = len(shape)
    return pl.BlockSpec(shape, lambda *_: (0,) * nd, pipeline_mode=pl.Buffered(1))


def _layer_spec(shape, layer):
    nd = len(shape)
    return pl.BlockSpec((1,) + shape, lambda *_: (layer,) + (0,) * nd, pipeline_mode=pl.Buffered(1))


def _params(n_axes):
    return pltpu.CompilerParams(dimension_semantics=("arbitrary",) * n_axes,
                                vmem_limit_bytes=VMEM_LIMIT)


def _mods_kernel(cv_ref, w_ref, b_ref, o_ref):
    cv = cv_ref[...]
    s = cv * _sigmoid(cv)
    o_ref[0] = jnp.dot(s, w_ref[0], preferred_element_type=F32, precision=HIGHEST) + b_ref[0]


def _mods(cv, w_mod, b_mod):
    tn = 1536
    n6 = 6 * D_MODEL
    return pl.pallas_call(
        _mods_kernel,
        grid=(DEPTH, n6 // tn),
        in_specs=[pl.BlockSpec((16, D_MODEL), lambda l, n: (0, 0)),
                  pl.BlockSpec((1, D_MODEL, tn), lambda l, n: (l, 0, n)),
                  pl.BlockSpec((1, 1, tn), lambda l, n: (l, 0, n))],
        out_specs=pl.BlockSpec((1, 16, tn), lambda l, n: (l, 0, n)),
        out_shape=jax.ShapeDtypeStruct((DEPTH, 16, n6), F32),
        compiler_params=_params(2),
        name="mods",
    )(cv, w_mod, b_mod.reshape(DEPTH, 1, n6))


def _even_in_chain(s, n_sub, tm, j, nt, x_ref, xp_ref, xn_ref, mod_ref, gn_ref, win_ref, mu_ref, wup_ref,
                   aup_ref, gup_ref, vec_ref, e_ref, sh_ref, dr_ref, mg_ref):
    te = tm + 2 * HALO
    rows = slice(s * tm, (s + 1) * tm)
    top = xp_ref[0] if s == 0 else x_ref[0, s * tm - HALO:s * tm, :]
    bot = xn_ref[0] if s == n_sub - 1 else x_ref[0, (s + 1) * tm:(s + 1) * tm + HALO, :]
    xe = jnp.concatenate([top, x_ref[0, rows, :], bot], axis=0)
    h32 = _norm_mod(xe, gn_ref[0:1, :], mod_ref[0, 0:1, :], mod_ref[0, 1:2, :])
    h = h32.astype(BF16)

    def proj(c0, c1):
        return jnp.dot(h, win_ref[:, c0:c1], preferred_element_type=F32)

    def shifted(pb, c0):
        c1 = c0 + pb.shape[1]
        pc = pb[HALO:tm + HALO]
        p_top = jnp.where(j == 0, 0.0, pb[:HALO]) if s == 0 else pb[:HALO]
        p_bot = jnp.where(j == nt - 1, 0.0, pb[tm + HALO:]) if s == n_sub - 1 else pb[tm + HALO:]
        pe = jnp.concatenate([p_top, pc, p_bot], axis=0)
        prev = pltpu.roll(pe, 1, 0)[HALO:tm + HALO]
        nxt = pltpu.roll(pe, te - 1, 0)[HALO:tm + HALO]
        mu0 = mu_ref[0:1, c0:c1]
        mu1 = mu_ref[1:2, c0:c1]
        return pc * (1.0 - mu0 - mu1) + mu0 * prev + mu1 * nxt

    p_k = proj(D_RWKV, 2 * D_RWKV)
    p_l = proj(C_LW, C_SHIFT)
    p_r = proj(0, D_RWKV)
    p_v = proj(2 * D_RWKV, 3 * D_RWKV)
    p_pool = jnp.dot(h32[HALO:tm + HALO].astype(BF16), win_ref[:, C_POOL:C_EVEN], preferred_element_type=F32)
    yield
    k = shifted(p_k, D_RWKV)
    lora = shifted(p_l, C_LW)
    wd = _bdot(jnp.tanh(lora[:, 0:LANES]), wup_ref[...])
    la = _bdot(lora[:, C_LA - C_LW:C_LA - C_LW + LANES], aup_ref[...])
    g = _bdot(_sigmoid(lora[:, C_LG - C_LW:C_LG - C_LW + LANES]), gup_ref[...])
    r = shifted(p_r, 0)
    v = shifted(p_v, 2 * D_RWKV)
    k_k = vec_ref[4:5, :]
    k_a = vec_ref[5:6, :]
    r_k = vec_ref[6:7, :]
    kkr = k * k_k
    nrm = _head_sums(kkr * kkr, e_ref[...])
    yield
    kk = kkr * lax.rsqrt(jnp.maximum(nrm, 1e-24))
    kd_sum = None
    for d in range(2):
        wdd = vec_ref[d:d + 1, :] + wd[:, d * D_RWKV:(d + 1) * D_RWKV]
        ld = -DECAY_SCALE * _sigmoid(wdd)
        a = _sigmoid(vec_ref[2 + d:3 + d, :] + la[:, d * D_RWKV:(d + 1) * D_RWKV])
        kd = k * (1.0 + (a - 1.0) * k_a)
        dr_ref[d, 0, rows, 0:D_RWKV] = ld
        dr_ref[d, 0, rows, D_RWKV:2 * D_RWKV] = kd
        dr_ref[d, 0, rows, 2 * D_RWKV:3 * D_RWKV] = kk * a
        kd_sum = kd if kd_sum is None else kd_sum + kd
    coef = _head_sums(r * kd_sum * r_k, e_ref[...])
    yield
    sh_ref[0, rows, 0:D_RWKV] = r
    sh_ref[0, rows, D_RWKV:2 * D_RWKV] = v
    sh_ref[0, rows, 2 * D_RWKV:3 * D_RWKV] = kk
    mg_ref[0, rows, 0:D_RWKV] = coef * v
    mg_ref[0, rows, D_RWKV:2 * D_RWKV] = g
    mg_ref[0, rows, 2 * D_RWKV:3 * D_RWKV] = p_pool


def _even_in_kernel(*refs, sub, n_sub, nt):
    j = pl.program_id(1)
    _round_robin([_even_in_chain(s, n_sub, sub, j, nt, *refs) for s in range(n_sub)])


def _even_in(x, mod, gn, ew):
    bsz, seq, _ = x.shape
    tm = min(TM_EVEN_IN, seq)
    sub = min(SUB_EVEN_IN, tm)
    assert seq % tm == 0 and tm % sub == 0
    nt = seq // tm
    hb = tm // HALO
    nhb = seq // HALO
    c3 = 3 * D_RWKV
    kern = functools.partial(_even_in_kernel, sub=sub, n_sub=tm // sub, nt=nt)
    mod_map = (lambda b, j: (b, 0, 0)) if mod.shape[0] == bsz else (lambda b, j: (0, 0, 0))
    return pl.pallas_call(
        kern,
        grid=(bsz, nt),
        in_specs=[pl.BlockSpec((1, tm, D_MODEL), lambda b, j: (b, j, 0)),
                  pl.BlockSpec((1, HALO, D_MODEL), lambda b, j: (b, jnp.maximum(j * hb - 1, 0), 0)),
                  pl.BlockSpec((1, HALO, D_MODEL), lambda b, j: (b, jnp.minimum((j + 1) * hb, nhb - 1), 0)),
                  pl.BlockSpec((1, 6, D_MODEL), mod_map),
                  _const_spec((4, D_MODEL)),
                  _const_spec((D_MODEL, C_EVEN)),
                  _const_spec((2, C_SHIFT)),
                  _const_spec((LANES, 2 * D_RWKV)),
                  _const_spec((LANES, 2 * D_RWKV)),
                  _const_spec((LANES, D_RWKV)),
                  _const_spec((16, D_RWKV)),
                  _const_spec((MXU_DIM, MXU_DIM))],
        out_specs=[pl.BlockSpec((1, tm, c3), lambda b, j: (b, j, 0)),
                   pl.BlockSpec((2, 1, tm, c3), lambda b, j: (0, b, j, 0)),
                   pl.BlockSpec((1, tm, c3), lambda b, j: (b, j, 0))],
        out_shape=[jax.ShapeDtypeStruct((bsz, seq, c3), F32),
                   jax.ShapeDtypeStruct((2, bsz, seq, c3), F32),
                   jax.ShapeDtypeStruct((bsz, seq, c3), F32)],
        compiler_params=_params(2),
        name="even_in",
    )(x, x, x, mod, gn, ew["w_in"], ew["mu"], ew["wup"], ew["aup"], ew["gup"], ew["vec"], ew["seg"])


def _mm(a, b):
    return jnp.dot(a.astype(BF16), b.astype(BF16), preferred_element_type=F32)


def _mm_nt(a, b):
    return lax.dot_general(a.astype(BF16), b.astype(BF16), (((1,), (1,)), ((), ())),
                           preferred_element_type=F32)


def _mm_tn(a, b):
    return lax.dot_general(a.astype(BF16), b.astype(BF16), (((0,), (0,)), ((), ())),
                           preferred_element_type=F32)


def _cumsum_rows(x, reverse, n):
    rows = lax.broadcasted_iota(jnp.int32, (n, 1), 0)
    s = 1
    while s < n:
        if s < SUBLANES:
            if reverse:
                shifted = jnp.where(rows < n - s, pltpu.roll(x, n - s, 0), 0.0)
            else:
                shifted = jnp.where(rows >= s, pltpu.roll(x, s, 0), 0.0)
        else:
            zeros = jnp.zeros((s, x.shape[1]), x.dtype)
            shifted = (jnp.concatenate([x[s:], zeros], axis=0) if reverse
                       else jnp.concatenate([zeros, x[:n - s]], axis=0))
        x = x + shifted
        s *= 2
    return x


def _wkv_masks(reverse, L):
    row = lax.broadcasted_iota(jnp.int32, (L, GW), 0)
    lane = lax.broadcasted_iota(jnp.int32, (L, GW), 1)
    colp = lane & (HEAD_DIM - 1)
    row2 = lax.broadcasted_iota(jnp.int32, (HEADS_PER_GROUP * L, GW), 0)
    lane2 = lax.broadcasted_iota(jnp.int32, (HEADS_PER_GROUP * L, GW), 1)
    blk = row ^ colp
    m = {
        "reverse": reverse,
        "head": lane // HEAD_DIM,
        "incl": (row <= colp) if reverse else (row >= colp),
        "strict": (row < colp) if reverse else (row > colp),
        "eye": (row == colp).astype(F32),
        "diag_blocks": (row2 // L) == (lane2 // HEAD_DIM),
        "same2": (blk >> 1) == 0,
    }
    g, lg = 2, 1
    while g < L:
        m["off%d" % g] = (blk >> lg) == 1
        g, lg = 2 * g, lg + 1
    return m


def _bd(x, head):
    return jnp.concatenate([jnp.where(head == h, x, 0.0) for h in range(HEADS_PER_GROUP)], axis=0)


def _pd(y, head, L):
    out = y[(HEADS_PER_GROUP - 1) * L:]
    for h in range(HEADS_PER_GROUP - 2, -1, -1):
        out = jnp.where(head == h, y[h * L:(h + 1) * L], out)
    return out


def _wkv_precompute(sh, dr, m, L):
    head = m["head"]
    reverse = m["reverse"]
    bd = lambda x: _bd(x, head)

    r = sh[:, 0:D_RWKV]
    v = sh[:, D_RWKV:2 * D_RWKV]
    kk = sh[:, 2 * D_RWKV:3 * D_RWKV]
    ld = dr[:, 0:D_RWKV]
    kd = dr[:, D_RWKV:2 * D_RWKV]
    b = dr[:, 2 * D_RWKV:3 * D_RWKV]

    cum = _cumsum_rows(ld, reverse, L)
    tot = cum[0:1, :] if reverse else cum[L - 1:L, :]
    kkt = kk * jnp.exp(cum - ld)
    rt = r * jnp.exp(cum)
    iw = jnp.exp(-cum)
    kh = kd * iw
    bh = b * iw
    wl = jnp.exp(tot - cum)
    khw = kd * wl
    bhw = b * wl
    wtot = jnp.exp(tot)

    def group_chain(p, res):
        sl = slice(p * GW, (p + 1) * GW)
        kkt_p, rt_p, kh_p, bh_p = kkt[:, sl], rt[:, sl], kh[:, sl], bh[:, sl]
        v_p, khw_p, bhw_p = v[:, sl], khw[:, sl], bhw[:, sl]
        lhs = jnp.concatenate([kkt_p, rt_p], axis=0)
        sc = _mm_nt(lhs, jnp.concatenate([bd(bh_p), bd(kh_p)], axis=0))
        yield
        sb = sc[:, :GW]
        sk = sc[:, GW:]
        a_ab = jnp.where(m["strict"], sb[:L], 0.0)
        a_rb = jnp.where(m["incl"], sb[L:], 0.0)
        a_ak = jnp.where(m["strict"], sk[:L], 0.0)
        a_rk = jnp.where(m["incl"], sk[L:], 0.0)
        avk = _mm(jnp.concatenate([a_ak, a_rk], axis=0), bd(v_p))
        t = m["eye"] - jnp.where(m["same2"], a_ab, 0.0)
        g = 2
        while g < L:
            w = _mm(t, bd(jnp.where(m["off%d" % g], a_ab, 0.0)))
            yield
            t = t - _mm(w, bd(t))
            yield
            g *= 2
        pu = _mm(t, jnp.concatenate([bd(kkt_p), bd(avk[:L])], axis=1))
        yield
        pm = pu[:, :GW]
        u0 = pu[:, GW:]
        qy = _mm(a_rb, jnp.concatenate([bd(pm), bd(u0)], axis=1))
        mc = jnp.where(m["diag_blocks"], _mm_tn(pm, bhw_p), 0.0)
        gg = _pd(_mm_tn(jnp.concatenate([v_p, u0], axis=0),
                        jnp.concatenate([khw_p, -bhw_p], axis=0)), head, L)
        q = rt_p - qy[:, :GW]
        y0 = avk[L:] - qy[:, GW:]
        res[p] = (q, y0, mc, gg, wtot[:, sl])

    res = [None] * N_GROUPS
    return res, [group_chain(p, res) for p in range(N_GROUPS)]


def _round_robin(chains):
    chains = list(chains)
    while chains:
        alive = []
        for ch in chains:
            try:
                next(ch)
                alive.append(ch)
            except StopIteration:
                pass
        chains = alive


def _wkv_kernel(shf_ref, shb_ref, drf_ref, drb_ref, s0_ref, yf_ref, yb_ref, sout_ref, s_ref,
                *, L, cps, nsteps):
    j = pl.program_id(1)

    @pl.when(j == 0)
    def _():
        s_ref[...] = s0_ref[0]

    io = ((shf_ref, drf_ref, yf_ref), (shb_ref, drb_ref, yb_ref))
    order = (list(range(cps)), list(range(cps - 1, -1, -1)))
    results, chains = {}, []
    for d, (sh_ref, dr_ref, _) in enumerate(io):
        m = _wkv_masks(d == 1, L)
        for c in order[d]:
            rows = slice(c * L, (c + 1) * L)
            results[d, c], ch = _wkv_precompute(sh_ref[0, rows, :], dr_ref[0, 0, rows, :], m, L)
            chains += ch
    _round_robin(chains)

    head = lax.broadcasted_iota(jnp.int32, (L, GW), 1) // HEAD_DIM
    state = [[s_ref[d, p] for p in range(N_GROUPS)] for d in range(2)]
    for i in range(cps):
        old = [[None] * N_GROUPS for _ in range(2)]
        for d in range(2):
            for p in range(N_GROUPS):
                _, _, mc, gg, wtot = results[d, order[d][i]][p]
                s0 = state[d][p]
                old[d][p] = s0
                state[d][p] = s0 * wtot - _mm(s0, mc) + gg
        for d in range(2):
            c = order[d][i]
            for p in range(N_GROUPS):
                q, y0 = results[d, c][p][:2]
                io[d][2][0, c * L:(c + 1) * L, p * GW:(p + 1) * GW] = _mm_nt(q, _bd(old[d][p], head)) + y0
    for d in range(2):
        for p in range(N_GROUPS):
            s_ref[d, p] = state[d][p]

    @pl.when(j == nsteps - 1)
    def _():
        sout_ref[0] = s_ref[...]


def _wkv(sh, dr, s0):
    bsz, seq, c3 = sh.shape
    L = CHUNK
    assert L == HEAD_DIM
    cps = min(CHUNKS_PER_STEP, seq // L)
    tb = cps * L
    assert seq % tb == 0
    ns = seq // tb
    kern = functools.partial(_wkv_kernel, L=L, cps=cps, nsteps=ns)
    st_shape = (1, 2, N_GROUPS, HEAD_DIM, GW)
    return pl.pallas_call(
        kern,
        grid=(bsz, ns),
        in_specs=[pl.BlockSpec((1, tb, c3), lambda b, j: (b, j, 0)),
                  pl.BlockSpec((1, tb, c3), lambda b, j: (b, ns - 1 - j, 0)),
                  pl.BlockSpec((1, 1, tb, c3), lambda b, j: (0, b, j, 0)),
                  pl.BlockSpec((1, 1, tb, c3), lambda b, j: (1, b, ns - 1 - j, 0)),
                  pl.BlockSpec(st_shape, lambda b, j: (b, 0, 0, 0, 0))],
        out_specs=[pl.BlockSpec((1, tb, D_RWKV), lambda b, j: (b, j, 0)),
                   pl.BlockSpec((1, tb, D_RWKV), lambda b, j: (b, ns - 1 - j, 0)),
                   pl.BlockSpec(st_shape, lambda b, j: (b, 0, 0, 0, 0))],
        out_shape=[jax.ShapeDtypeStruct((bsz, seq, D_RWKV), F32),
                   jax.ShapeDtypeStruct((bsz, seq, D_RWKV), F32),
                   jax.ShapeDtypeStruct((bsz, 2, N_GROUPS, HEAD_DIM, GW), F32)],
        scratch_shapes=[pltpu.VMEM((2, N_GROUPS, HEAD_DIM, GW), F32)],
        compiler_params=_params(2),
        name="wkv",
    )(sh, sh, dr, dr, s0)


def _ffn_chain(x, mod_ref, gn_ref, wup_ref, cw_ref, wdn_ref, tm, row_len, store):
    h = _norm_mod(x, gn_ref[2:3, :], mod_ref[0, 3:4, :], mod_ref[0, 4:5, :]).astype(BF16)
    first, last = _row_edges(tm, row_len)
    yo = None
    for c0, c1 in FF_SLABS:
        a = jnp.dot(h, wup_ref[0, :, c0:c1], preferred_element_type=F32)
        gate = jnp.dot(h, wup_ref[0, :, D_FF + c0:D_FF + c1], preferred_element_type=F32)
        yield
        c = _dwconv3(a, cw_ref[0, :, c0:c1], first, last, tm)
        u = (c * _sigmoid(c)) * gate
        part = jnp.dot(u.astype(BF16), wdn_ref[0, c0:c1, :], preferred_element_type=F32)
        yield
        yo = part if yo is None else yo + part
    store(x + mod_ref[0, 5:6, :] * _rms(yo, gn_ref[3:4, :]))


def _ffn_specs(layer):
    return [_layer_spec((D_MODEL, 2 * D_FF), layer), _layer_spec((3, D_FF), layer),
            _layer_spec((D_FF, D_MODEL), layer)]


def _token_tiling(x, mod, row_len):
    bsz, seq, _ = x.shape
    tm = min(TM_MIX, seq)
    sub = min(SUB_MIX, tm)
    assert seq % tm == 0 and tm % sub == 0 and sub % row_len == 0 and row_len & (row_len - 1) == 0
    assert all(c % MXU_DIM == 0 for c, _ in FF_SLABS) and FF_SLABS[-1][1] == D_FF and D_RWKV % MXU_DIM == 0
    mod_map = (lambda b, j: (b, 0, 0)) if mod.shape[0] == bsz else (lambda b, j: (0, 0, 0))
    return bsz, seq // tm, tm, sub, pl.BlockSpec((1, 6, D_MODEL), mod_map)


def _even_out_chain(rows, x_ref, yf_ref, yb_ref, mg_ref, mod_ref, gn_ref, vec_ref, e_ref, band_ref,
                    icnt_ref, pw_ref, wout_ref, wup_ref, cw_ref, wdn_ref, o_ref, tm, pt, row_len):
    y = yf_ref[0, rows, :] + yb_ref[0, rows, :]
    inv_n = 1.0 / HEAD_DIM
    mean = _head_sums(y, e_ref[...]) * inv_n
    yield
    yc = y - mean
    var = _head_sums(yc * yc, e_ref[...]) * inv_n
    yield
    yn = yc * lax.rsqrt(var + GN_EPS) * vec_ref[7:8, :] + vec_ref[8:9, :]
    y_rwkv = (yn + mg_ref[0, rows, 0:D_RWKV]) * mg_ref[0, rows, D_RWKV:2 * D_RWKV]

    pp = mg_ref[0, rows, 2 * D_RWKV:3 * D_RWKV]
    outs = []
    for gi in range(len(POOL_WINDOWS)):
        sl = slice(gi * POOL_GROUP, (gi + 1) * POOL_GROUP)
        xg = pp[:, sl]
        wsum = jnp.concatenate([_split_dot_lhs01(band_ref[gi], xg[s * pt:(s + 1) * pt])
                                for s in range(tm // pt)], axis=0)
        icnt = jnp.concatenate([icnt_ref[:, sl]] * (tm // pt), axis=0)
        dg = wsum * icnt - xg
        outs.append(_bdot(dg, pw_ref[gi]))
    yield
    y_pool = jnp.concatenate(outs, axis=-1) * vec_ref[9:10, :]

    cat = jnp.concatenate([y_rwkv, y_pool], axis=-1)
    yo = _bdot(cat, wout_ref[...])
    yield
    x1 = x_ref[0, rows, :] + mod_ref[0, 2:3, :] * _rms(yo, gn_ref[1:2, :])

    def store(val):
        o_ref[0, rows, :] = val

    yield from _ffn_chain(x1, mod_ref, gn_ref, wup_ref, cw_ref, wdn_ref, tm, row_len, store)


def _even_out_kernel(*refs, tm, sub, pt, row_len):
    _round_robin([_even_out_chain(slice(s * sub, (s + 1) * sub), *refs, sub, pt, row_len)
                  for s in range(tm // sub)])


def _even_out_ffn(x, yf, yb, mg, mod, gn, ew, pc, fw, row_len):
    bsz, nt, tm, sub, mod_spec = _token_tiling(x, mod, row_len)
    c3 = 3 * D_RWKV
    tok = lambda w: pl.BlockSpec((1, tm, w), lambda b, j: (b, j, 0))
    pt = pc["band"].shape[1]
    assert sub % pt == 0
    return pl.pallas_call(
        functools.partial(_even_out_kernel, tm=tm, sub=sub, pt=pt, row_len=row_len),
        grid=(bsz, nt),
        in_specs=[tok(D_MODEL), tok(D_RWKV), tok(D_RWKV), tok(c3), mod_spec,
                  _const_spec((4, D_MODEL)),
                  _const_spec((16, D_RWKV)),
                  _const_spec((MXU_DIM, MXU_DIM)),
                  _const_spec((4, pt, pt)),
                  _const_spec((pt, D_POOL)),
                  _const_spec((4, POOL_GROUP, POOL_GROUP)),
                  _const_spec((D_MODEL, D_MODEL))] + _ffn_specs(fw["layer"]),
        out_specs=tok(D_MODEL),
        out_shape=jax.ShapeDtypeStruct(x.shape, F32),
        compiler_params=_params(2),
        name="even_out_ffn",
    )(x, yf, yb, mg, mod, gn, ew["vec"], ew["seg"], pc["band"], pc["icnt"], ew["pool_w"], ew["w_out"],
      fw["w_up"], fw["conv"], fw["w_down"])


def _odd_chain(rows, x_ref, mod_ref, gn_ref, win_ref, cw_ref, wout_ref, fup_ref, fcw_ref, fdn_ref, o_ref,
               tm, row_len):
    x = x_ref[0, rows, :]
    h = _norm_mod(x, gn_ref[0:1, :], mod_ref[0, 0:1, :], mod_ref[0, 1:2, :])
    p = jnp.dot(h.astype(BF16), win_ref[0], preferred_element_type=F32)
    yield
    bg = p[:, 0:D_MODEL]
    cg = p[:, D_MODEL:2 * D_MODEL]
    u = p[:, 2 * D_MODEL:3 * D_MODEL]
    first, last = _row_edges(tm, row_len)
    z = bg * _dwconv3(cg * u, cw_ref[0], first, last, tm)
    yo = _bdot(z, wout_ref[0])
    yield
    x1 = x + mod_ref[0, 2:3, :] * _rms(yo, gn_ref[1:2, :])

    def store(val):
        o_ref[0, rows, :] = val

    yield from _ffn_chain(x1, mod_ref, gn_ref, fup_ref, fcw_ref, fdn_ref, tm, row_len, store)


def _odd_kernel(*refs, tm, sub, row_len):
    _round_robin([_odd_chain(slice(s * sub, (s + 1) * sub), *refs, sub, row_len) for s in range(tm // sub)])


def _odd_mix_ffn(x, mod, gn, ow, fw, row_len):
    bsz, nt, tm, sub, mod_spec = _token_tiling(x, mod, row_len)
    tok = pl.BlockSpec((1, tm, D_MODEL), lambda b, j: (b, j, 0))
    return pl.pallas_call(
        functools.partial(_odd_kernel, tm=tm, sub=sub, row_len=row_len),
        grid=(bsz, nt),
        in_specs=[tok, mod_spec,
                  _const_spec((4, D_MODEL)),
                  _layer_spec((D_MODEL, 3 * D_MODEL), ow["layer"]),
                  _layer_spec((3, D_MODEL), ow["layer"]),
                  _layer_spec((D_MODEL, D_MODEL), ow["layer"])] + _ffn_specs(fw["layer"]),
        out_specs=tok,
        out_shape=jax.ShapeDtypeStruct(x.shape, F32),
        compiler_params=_params(2),
        name="odd_mix_ffn",
    )(x, mod, gn, ow["w_in"], ow["conv"], ow["w_out"], fw["w_up"], fw["conv"], fw["w_down"])


def _pool_consts(tm, row_len):
    pos = np.arange(tm)
    col = pos % row_len
    same_row = (pos[:, None] // row_len) == (pos[None, :] // row_len)
    band = np.zeros((len(POOL_WINDOWS), tm, tm), np.float32)
    icnt = np.zeros((tm, D_POOL), np.float32)
    for gi, win in enumerate(POOL_WINDOWS):
        lo = np.clip(col - win // 2, 0, row_len)
        hi = np.clip(col + win // 2, 0, row_len)
        inside = (col[None, :] >= lo[:, None]) & (col[None, :] < hi[:, None]) & same_row
        band[gi] = inside.astype(np.float32)
        icnt[:, gi * POOL_GROUP:(gi + 1) * POOL_GROUP] = (1.0 / (hi - lo).astype(np.float32))[:, None]
    return {"band": jnp.asarray(band, BF16), "icnt": jnp.asarray(icnt, F32)}


def _even_weights(i, ev_w_in, ev_w_out, ev_mu, ev_w0, ev_w_up, ev_a0, ev_a_up, ev_g_up, ev_k_k,
                  ev_k_a, ev_r_k, ev_gn_w, ev_gn_b, ev_pool_w, ev_pool_scale):
    o_lw = 3 * D_RWKV
    o_la = o_lw + 2 * D_DECAY_LORA
    o_lg = o_la + 2 * D_AAA_LORA
    o_pool = o_lg + D_GATE_LORA

    def repack(a, with_pool):
        z = lambda n: jnp.zeros(a.shape[:-1] + (n,), a.dtype)
        parts = [a[..., :o_lw], a[..., o_lw:o_la], z(LANES - 2 * D_DECAY_LORA),
                 a[..., o_la:o_lg], a[..., o_lg:o_pool], z(LANES - D_GATE_LORA)]
        if with_pool:
            parts.append(a[..., o_pool:])
        return jnp.concatenate(parts, axis=-1)

    def per_dir(w):
        rows = jnp.concatenate([jnp.pad(w[0], ((0, 0), (0, D_RWKV))), jnp.pad(w[1], ((0, 0), (D_RWKV, 0)))], axis=0)
        return jnp.pad(rows, ((0, LANES - rows.shape[0]), (0, 0)))

    wup = per_dir(ev_w_up[i])
    aup = per_dir(ev_a_up[i])
    gup = jnp.pad(ev_g_up[i], ((0, LANES - D_GATE_LORA), (0, 0)))
    row = lambda a: a.reshape(1, D_RWKV)
    vec = jnp.concatenate([ev_w0[i], ev_a0[i], row(ev_k_k[i]), row(ev_k_a[i]), row(ev_r_k[i]), row(ev_gn_w[i]),
                           row(ev_gn_b[i]), row(ev_pool_scale[i]), jnp.zeros((6, D_RWKV), F32)], axis=0)
    head = np.arange(MXU_DIM) // HEAD_DIM
    seg = jnp.asarray(head[:, None] == head[None, :], BF16)
    return {"w_in": repack(ev_w_in[i], True).astype(BF16), "mu": repack(ev_mu[i], False),
            "wup": wup.astype(BF16), "aup": aup.astype(BF16), "gup": gup.astype(BF16),
            "vec": vec, "seg": seg, "pool_w": ev_pool_w[i].astype(BF16),
            "w_out": ev_w_out[i].astype(BF16)}


def kernel(x, c, ctx, c_ctx, w_mod, b_mod, norm_g, ffn_w_up, ffn_conv, ffn_w_down, ev_w_in, ev_w_out,
           ev_mu, ev_w0, ev_w_up, ev_a0, ev_a_up, ev_g_up, ev_k_k, ev_k_a, ev_r_k, ev_gn_w, ev_gn_b,
           ev_pool_w, ev_pool_scale, od_w_in, od_conv, od_w_out):
    bsz, seq, _ = x.shape
    ctx_len = ctx.shape[1]
    assert SUB_MIX % POOL_TILE == 0 and POOL_TILE % GRID_W == 0
    assert ctx_len <= min(SUB_MIX, TM_EVEN_IN) and ctx_len % CHUNK == 0 and seq % (CHUNK * CHUNKS_PER_STEP) == 0 and bsz < 16

    cv = jnp.concatenate([c, c_ctx[None], jnp.zeros((15 - bsz, D_MODEL), F32)], axis=0)
    mods = _mods(cv, w_mod, b_mod)
    pc_lat = _pool_consts(POOL_TILE, GRID_W)
    pc_ctx = _pool_consts(ctx_len, ctx_len)
    ffn_w_up_b = ffn_w_up.astype(BF16)
    ffn_w_down_b = ffn_w_down.astype(BF16)
    od_w_in_b = od_w_in.astype(BF16)
    od_w_out_b = od_w_out.astype(BF16)

    for layer in range(DEPTH):
        i = layer // 2
        even = layer % 2 == 0
        ctx_later = any(jj % 2 == 0 for jj in range(layer + 1, DEPTH))
        mod = mods[layer, :bsz].reshape(bsz, 6, D_MODEL)
        mod_c = mods[layer, bsz:bsz + 1].reshape(1, 6, D_MODEL)
        gn = norm_g[layer]
        fw = {"w_up": ffn_w_up_b, "conv": ffn_conv, "w_down": ffn_w_down_b, "layer": layer}
        if even:
            ew = _even_weights(i, ev_w_in, ev_w_out, ev_mu, ev_w0, ev_w_up, ev_a0, ev_a_up, ev_g_up,
                               ev_k_k, ev_k_a, ev_r_k, ev_gn_w, ev_gn_b, ev_pool_w, ev_pool_scale)
            sh_c, dr_c, mg_c = _even_in(ctx, mod_c, gn, ew)
            sh_l, dr_l, mg_l = _even_in(x, mod, gn, ew)
            zero = jnp.zeros((bsz, 2, N_GROUPS, HEAD_DIM, GW), F32)
            yf_c, yb_c, s_c = _wkv(sh_c, dr_c, zero)
            yf_l, yb_l, _ = _wkv(sh_l, dr_l, s_c)
            x = _even_out_ffn(x, yf_l, yb_l, mg_l, mod, gn, ew, pc_lat, fw, GRID_W)
            if ctx_later:
                ctx = _even_out_ffn(ctx, yf_c, yb_c, mg_c, mod_c, gn, ew, pc_ctx, fw, ctx_len)
        else:
            ow = {"w_in": od_w_in_b, "conv": od_conv, "w_out": od_w_out_b, "layer": i}
            x = _odd_mix_ffn(x, mod, gn, ow, fw, GRID_W)
            if ctx_later:
                ctx = _odd_mix_ffn(ctx, mod_c, gn, ow, fw, ctx_len)
    return x
```

```python
import functools
import math

import numpy as np
import jax
import jax.numpy as jnp
from jax import lax
from jax.experimental import pallas as pl
from jax.experimental.pallas import tpu as pltpu

D_MODEL = 1024
DEPTH = 4
GRID_W = 64
HEAD_DIM = 64
N_HEADS = 8
D_RWKV = N_HEADS * HEAD_DIM
D_POOL = D_MODEL - D_RWKV
POOL_WINDOWS = (2, 4, 8, 16)
POOL_GROUP = D_POOL // len(POOL_WINDOWS)
D_DECAY_LORA = 32
D_AAA_LORA = 64
D_GATE_LORA = 96
D_FF = 2816
RMS_EPS = 1e-6
GN_EPS = 64e-5
DECAY_SCALE = math.exp(-0.5)

LANES = 128
SUBLANES = 8
MXU_DIM = 256
HEADS_PER_GROUP = 2
GW = HEADS_PER_GROUP * HEAD_DIM
N_GROUPS = N_HEADS // HEADS_PER_GROUP

C_LW = 3 * D_RWKV
C_LA = C_LW + LANES
C_LG = C_LA + LANES
C_SHIFT = C_LG + LANES
C_POOL = C_SHIFT
C_EVEN = C_POOL + D_POOL

HALO = SUBLANES
CHUNK = 64
CHUNKS_PER_STEP = 8
TM_EVEN_IN = 512
SUB_EVEN_IN = 512
TM_MIX = 512
SUB_MIX = 256
POOL_TILE = 256
FF_SLABS = ((0, D_FF),)

F32 = jnp.float32
BF16 = jnp.bfloat16
HIGHEST = lax.Precision.HIGHEST
VMEM_LIMIT = 56 * 1024 * 1024


def _sigmoid(x):
    return 1.0 / (1.0 + jnp.exp(-x))


def _rms(x, g):
    ms = jnp.mean(x * x, axis=-1, keepdims=True)
    return x * lax.rsqrt(ms + RMS_EPS) * g


def _norm_mod(x, g, shift, scale):
    return _rms(x, g) * (1.0 + scale) + shift


def _bdot(a, b):
    return jnp.dot(a.astype(BF16), b.astype(BF16), preferred_element_type=F32)


def _head_sums(x, e):
    m, width = x.shape
    w = e.shape[0]
    xs = jnp.concatenate([x[:, i * w:(i + 1) * w] for i in range(width // w)], axis=0)
    s = jnp.dot(xs.astype(BF16), e, preferred_element_type=F32)
    return jnp.concatenate([s[i * m:(i + 1) * m] for i in range(width // w)], axis=1)


def _window_sums(x, tm, row_len):
    col = lax.broadcasted_iota(jnp.int32, (tm, 1), 0) & (row_len - 1)

    def down(a, k):
        return jnp.where(col >= k, pltpu.roll(a, k, 0), 0.0)

    def up(a, k):
        return jnp.where(col < row_len - k, pltpu.roll(a, tm - k, 0), 0.0)

    f = b = x
    outs = []
    for gi, win in enumerate(POOL_WINDOWS):
        m = win // 2
        if gi > 0:
            f = f[:, POOL_GROUP:]
            b = b[:, POOL_GROUP:]
            f = f + down(f, m // 2)
            b = b + up(b, m // 2)
        outs.append(down(f[:, :POOL_GROUP], 1) + b[:, :POOL_GROUP])
    return jnp.concatenate(outs, axis=1)


def _row_edges(tm, row_len):
    pos = lax.broadcasted_iota(jnp.int32, (tm, 1), 0) & (row_len - 1)
    return pos == 0, pos == row_len - 1


def _dwconv3(x, w, first, last, tm):
    prev = jnp.where(first, 0.0, pltpu.roll(x, 1, 0))
    nxt = jnp.where(last, 0.0, pltpu.roll(x, tm - 1, 0))
    return prev * w[0:1, :] + x * w[1:2, :] + nxt * w[2:3, :]


def _const_spec(shape):
    nd = len(shape)
    return pl.BlockSpec(shape, lambda *_: (0,) * nd, pipeline_mode=pl.Buffered(1))


def _layer_spec(shape, layer):
    nd = len(shape)
    return pl.BlockSpec((1,) + shape, lambda *_: (layer,) + (0,) * nd, pipeline_mode=pl.Buffered(1))


def _params(n_axes):
    return pltpu.CompilerParams(dimension_semantics=("arbitrary",) * n_axes,
                                vmem_limit_bytes=VMEM_LIMIT)


def _mods_kernel(cv_ref, w_ref, b_ref, o_ref):
    cv = cv_ref[...]
    s = cv * _sigmoid(cv)
    o_ref[0] = jnp.dot(s, w_ref[0], preferred_element_type=F32, precision=HIGHEST) + b_ref[0]


def _mods(cv, w_mod, b_mod):
    tn = 1536
    n6 = 6 * D_MODEL
    return pl.pallas_call(
        _mods_kernel,
        grid=(DEPTH, n6 // tn),
        in_specs=[pl.BlockSpec((16, D_MODEL), lambda l, n: (0, 0)),
                  pl.BlockSpec((1, D_MODEL, tn), lambda l, n: (l, 0, n)),
                  pl.BlockSpec((1, 1, tn), lambda l, n: (l, 0, n))],
        out_specs=pl.BlockSpec((1, 16, tn), lambda l, n: (l, 0, n)),
        out_shape=jax.ShapeDtypeStruct((DEPTH, 16, n6), F32),
        compiler_params=_params(2),
        name="mods",
    )(cv, w_mod, b_mod.reshape(DEPTH, 1, n6))


def _even_in_chain(s, n_sub, tm, j, nt, x_ref, xp_ref, xn_ref, mod_ref, gn_ref, win_ref, mu_ref, wup_ref,
                   aup_ref, gup_ref, vec_ref, e_ref, sh_ref, dr_ref, mg_ref):
    te = tm + 2 * HALO
    rows = slice(s * tm, (s + 1) * tm)
    top = xp_ref[0] if s == 0 else x_ref[0, s * tm - HALO:s * tm, :]
    bot = xn_ref[0] if s == n_sub - 1 else x_ref[0, (s + 1) * tm:(s + 1) * tm + HALO, :]
    xe = jnp.concatenate([top, x_ref[0, rows, :], bot], axis=0)
    h32 = _norm_mod(xe, gn_ref[0:1, :], mod_ref[0, 0:1, :], mod_ref[0, 1:2, :])
    h = h32.astype(BF16)

    def proj(c0, c1):
        return jnp.dot(h, win_ref[:, c0:c1], preferred_element_type=F32)

    def shifted(pb, c0):
        c1 = c0 + pb.shape[1]
        pc = pb[HALO:tm + HALO]
        p_top = jnp.where(j == 0, 0.0, pb[:HALO]) if s == 0 else pb[:HALO]
        p_bot = jnp.where(j == nt - 1, 0.0, pb[tm + HALO:]) if s == n_sub - 1 else pb[tm + HALO:]
        pe = jnp.concatenate([p_top, pc, p_bot], axis=0)
        prev = pltpu.roll(pe, 1, 0)[HALO:tm + HALO]
        nxt = pltpu.roll(pe, te - 1, 0)[HALO:tm + HALO]
        mu0 = mu_ref[0:1, c0:c1]
        mu1 = mu_ref[1:2, c0:c1]
        return pc * (1.0 - mu0 - mu1) + mu0 * prev + mu1 * nxt

    p_k = proj(D_RWKV, 2 * D_RWKV)
    p_l = proj(C_LW, C_SHIFT)
    p_r = proj(0, D_RWKV)
    p_v = proj(2 * D_RWKV, 3 * D_RWKV)
    p_pool = jnp.dot(h32[HALO:tm + HALO].astype(BF16), win_ref[:, C_POOL:C_EVEN], preferred_element_type=F32)
    yield
    k = shifted(p_k, D_RWKV)
    lora = shifted(p_l, C_LW)
    wd = _bdot(jnp.tanh(lora[:, 0:LANES]), wup_ref[...])
    la = _bdot(lora[:, C_LA - C_LW:C_LA - C_LW + LANES], aup_ref[...])
    g = _bdot(_sigmoid(lora[:, C_LG - C_LW:C_LG - C_LW + LANES]), gup_ref[...])
    r = shifted(p_r, 0)
    v = shifted(p_v, 2 * D_RWKV)
    k_k = vec_ref[4:5, :]
    k_a = vec_ref[5:6, :]
    r_k = vec_ref[6:7, :]
    kkr = k * k_k
    nrm = _head_sums(kkr * kkr, e_ref[...])
    yield
    kk = kkr * lax.rsqrt(jnp.maximum(nrm, 1e-24))
    kd_sum = None
    for d in range(2):
        wdd = vec_ref[d:d + 1, :] + wd[:, d * D_RWKV:(d + 1) * D_RWKV]
        ld = -DECAY_SCALE * _sigmoid(wdd)
        a = _sigmoid(vec_ref[2 + d:3 + d, :] + la[:, d * D_RWKV:(d + 1) * D_RWKV])
        kd = k * (1.0 + (a - 1.0) * k_a)
        dr_ref[d, 0, rows, 0:D_RWKV] = ld
        dr_ref[d, 0, rows, D_RWKV:2 * D_RWKV] = kd
        dr_ref[d, 0, rows, 2 * D_RWKV:3 * D_RWKV] = kk * a
        kd_sum = kd if kd_sum is None else kd_sum + kd
    coef = _head_sums(r * kd_sum * r_k, e_ref[...])
    yield
    sh_ref[0, rows, 0:D_RWKV] = r
    sh_ref[0, rows, D_RWKV:2 * D_RWKV] = v
    sh_ref[0, rows, 2 * D_RWKV:3 * D_RWKV] = kk
    mg_ref[0, rows, 0:D_RWKV] = coef * v
    mg_ref[0, rows, D_RWKV:2 * D_RWKV] = g
    mg_ref[0, rows, 2 * D_RWKV:3 * D_RWKV] = p_pool


def _even_in_kernel(*refs, sub, n_sub, nt):
    j = pl.program_id(1)
    _round_robin([_even_in_chain(s, n_sub, sub, j, nt, *refs) for s in range(n_sub)])


def _even_in(x, mod, gn, ew):
    bsz, seq, _ = x.shape
    tm = min(TM_EVEN_IN, seq)
    sub = min(SUB_EVEN_IN, tm)
    assert seq % tm == 0 and tm % sub == 0
    nt = seq // tm
    hb = tm // HALO
    nhb = seq // HALO
    c3 = 3 * D_RWKV
    kern = functools.partial(_even_in_kernel, sub=sub, n_sub=tm // sub, nt=nt)
    mod_map = (lambda b, j: (b, 0, 0)) if mod.shape[0] == bsz else (lambda b, j: (0, 0, 0))
    return pl.pallas_call(
        kern,
        grid=(bsz, nt),
        in_specs=[pl.BlockSpec((1, tm, D_MODEL), lambda b, j: (b, j, 0)),
                  pl.BlockSpec((1, HALO, D_MODEL), lambda b, j: (b, jnp.maximum(j * hb - 1, 0), 0)),
                  pl.BlockSpec((1, HALO, D_MODEL), lambda b, j: (b, jnp.minimum((j + 1) * hb, nhb - 1), 0)),
                  pl.BlockSpec((1, 6, D_MODEL), mod_map),
                  _const_spec((4, D_MODEL)),
                  _const_spec((D_MODEL, C_EVEN)),
                  _const_spec((2, C_SHIFT)),
                  _const_spec((LANES, 2 * D_RWKV)),
                  _const_spec((LANES, 2 * D_RWKV)),
                  _const_spec((LANES, D_RWKV)),
                  _const_spec((16, D_RWKV)),
                  _const_spec((MXU_DIM, MXU_DIM))],
        out_specs=[pl.BlockSpec((1, tm, c3), lambda b, j: (b, j, 0)),
                   pl.BlockSpec((2, 1, tm, c3), lambda b, j: (0, b, j, 0)),
                   pl.BlockSpec((1, tm, c3), lambda b, j: (b, j, 0))],
        out_shape=[jax.ShapeDtypeStruct((bsz, seq, c3), F32),
                   jax.ShapeDtypeStruct((2, bsz, seq, c3), F32),
                   jax.ShapeDtypeStruct((bsz, seq, c3), F32)],
        compiler_params=_params(2),
        name="even_in",
    )(x, x, x, mod, gn, ew["w_in"], ew["mu"], ew["wup"], ew["aup"], ew["gup"], ew["vec"], ew["seg"])


def _mm(a, b):
    return jnp.dot(a.astype(BF16), b.astype(BF16), preferred_element_type=F32)


def _mm_nt(a, b):
    return lax.dot_general(a.astype(BF16), b.astype(BF16), (((1,), (1,)), ((), ())),
                           preferred_element_type=F32)


def _mm_tn(a, b):
    return lax.dot_general(a.astype(BF16), b.astype(BF16), (((0,), (0,)), ((), ())),
                           preferred_element_type=F32)


def _cumsum_rows(x, reverse, n):
    rows = lax.broadcasted_iota(jnp.int32, (n, 1), 0)
    s = 1
    while s < n:
        if s < SUBLANES:
            if reverse:
                shifted = jnp.where(rows < n - s, pltpu.roll(x, n - s, 0), 0.0)
            else:
                shifted = jnp.where(rows >= s, pltpu.roll(x, s, 0), 0.0)
        else:
            zeros = jnp.zeros((s, x.shape[1]), x.dtype)
            shifted = (jnp.concatenate([x[s:], zeros], axis=0) if reverse
                       else jnp.concatenate([zeros, x[:n - s]], axis=0))
        x = x + shifted
        s *= 2
    return x


def _wkv_masks(reverse, L):
    row = lax.broadcasted_iota(jnp.int32, (L, GW), 0)
    lane = lax.broadcasted_iota(jnp.int32, (L, GW), 1)
    colp = lane & (HEAD_DIM - 1)
    row2 = lax.broadcasted_iota(jnp.int32, (HEADS_PER_GROUP * L, GW), 0)
    lane2 = lax.broadcasted_iota(jnp.int32, (HEADS_PER_GROUP * L, GW), 1)
    blk = row ^ colp
    m = {
        "reverse": reverse,
        "head": lane // HEAD_DIM,
        "incl": (row <= colp) if reverse else (row >= colp),
        "strict": (row < colp) if reverse else (row > colp),
        "eye": (row == colp).astype(F32),
        "diag_blocks": (row2 // L) == (lane2 // HEAD_DIM),
        "same2": (blk >> 1) == 0,
    }
    g, lg = 2, 1
    while g < L:
        m["off%d" % g] = (blk >> lg) == 1
        g, lg = 2 * g, lg + 1
    return m


def _bd(x, head):
    return jnp.concatenate([jnp.where(head == h, x, 0.0) for h in range(HEADS_PER_GROUP)], axis=0)


def _pd(y, head, L):
    out = y[(HEADS_PER_GROUP - 1) * L:]
    for h in range(HEADS_PER_GROUP - 2, -1, -1):
        out = jnp.where(head == h, y[h * L:(h + 1) * L], out)
    return out


def _wkv_precompute(sh, dr, m, L):
    head = m["head"]
    reverse = m["reverse"]
    bd = lambda x: _bd(x, head)

    r = sh[:, 0:D_RWKV]
    v = sh[:, D_RWKV:2 * D_RWKV]
    kk = sh[:, 2 * D_RWKV:3 * D_RWKV]
    ld = dr[:, 0:D_RWKV]
    kd = dr[:, D_RWKV:2 * D_RWKV]
    b = dr[:, 2 * D_RWKV:3 * D_RWKV]

    cum = _cumsum_rows(ld, reverse, L)
    tot = cum[0:1, :] if reverse else cum[L - 1:L, :]
    kkt = kk * jnp.exp(cum - ld)
    rt = r * jnp.exp(cum)
    iw = jnp.exp(-cum)
    kh = kd * iw
    bh = b * iw
    wl = jnp.exp(tot - cum)
    khw = kd * wl
    bhw = b * wl
    wtot = jnp.exp(tot)

    def group_chain(p, res):
        sl = slice(p * GW, (p + 1) * GW)
        kkt_p, rt_p, kh_p, bh_p = kkt[:, sl], rt[:, sl], kh[:, sl], bh[:, sl]
        v_p, khw_p, bhw_p = v[:, sl], khw[:, sl], bhw[:, sl]
        lhs = jnp.concatenate([kkt_p, rt_p], axis=0)
        sc = _mm_nt(lhs, jnp.concatenate([bd(bh_p), bd(kh_p)], axis=0))
        yield
        sb = sc[:, :GW]
        sk = sc[:, GW:]
        a_ab = jnp.where(m["strict"], sb[:L], 0.0)
        a_rb = jnp.where(m["incl"], sb[L:], 0.0)
        a_ak = jnp.where(m["strict"], sk[:L], 0.0)
        a_rk = jnp.where(m["incl"], sk[L:], 0.0)
        avk = _mm(jnp.concatenate([a_ak, a_rk], axis=0), bd(v_p))
        t = m["eye"] - jnp.where(m["same2"], a_ab, 0.0)
        g = 2
        while g < L:
            w = _mm(t, bd(jnp.where(m["off%d" % g], a_ab, 0.0)))
            yield
            t = t - _mm(w, bd(t))
            yield
            g *= 2
        pu = _mm(t, jnp.concatenate([bd(kkt_p), bd(avk[:L])], axis=1))
        yield
        pm = pu[:, :GW]
        u0 = pu[:, GW:]
        qy = _mm(a_rb, jnp.concatenate([bd(pm), bd(u0)], axis=1))
        mc = jnp.where(m["diag_blocks"], _mm_tn(pm, bhw_p), 0.0)
        gg = _pd(_mm_tn(jnp.concatenate([v_p, u0], axis=0),
                        jnp.concatenate([khw_p, -bhw_p], axis=0)), head, L)
        q = rt_p - qy[:, :GW]
        y0 = avk[L:] - qy[:, GW:]
        res[p] = (q, y0, mc, gg, wtot[:, sl])

    res = [None] * N_GROUPS
    return res, [group_chain(p, res) for p in range(N_GROUPS)]


def _round_robin(chains):
    chains = list(chains)
    while chains:
        alive = []
        for ch in chains:
            try:
                next(ch)
                alive.append(ch)
            except StopIteration:
                pass
        chains = alive


def _wkv_kernel(shf_ref, shb_ref, drf_ref, drb_ref, s0_ref, yf_ref, yb_ref, sout_ref, s_ref,
                *, L, cps, nsteps):
    j = pl.program_id(1)

    @pl.when(j == 0)
    def _():
        s_ref[...] = s0_ref[0]

    io = ((shf_ref, drf_ref, yf_ref), (shb_ref, drb_ref, yb_ref))
    order = (list(range(cps)), list(range(cps - 1, -1, -1)))
    results, chains = {}, []
    for d, (sh_ref, dr_ref, _) in enumerate(io):
        m = _wkv_masks(d == 1, L)
        for c in order[d]:
            rows = slice(c * L, (c + 1) * L)
            results[d, c], ch = _wkv_precompute(sh_ref[0, rows, :], dr_ref[0, 0, rows, :], m, L)
            chains += ch
    _round_robin(chains)

    head = lax.broadcasted_iota(jnp.int32, (L, GW), 1) // HEAD_DIM
    state = [[s_ref[d, p] for p in range(N_GROUPS)] for d in range(2)]
    for i in range(cps):
        old = [[None] * N_GROUPS for _ in range(2)]
        for d in range(2):
            for p in range(N_GROUPS):
                _, _, mc, gg, wtot = results[d, order[d][i]][p]
                s0 = state[d][p]
                old[d][p] = s0
                state[d][p] = s0 * wtot - _mm(s0, mc) + gg
        for d in range(2):
            c = order[d][i]
            for p in range(N_GROUPS):
                q, y0 = results[d, c][p][:2]
                io[d][2][0, c * L:(c + 1) * L, p * GW:(p + 1) * GW] = _mm_nt(q, _bd(old[d][p], head)) + y0
    for d in range(2):
        for p in range(N_GROUPS):
            s_ref[d, p] = state[d][p]

    @pl.when(j == nsteps - 1)
    def _():
        sout_ref[0] = s_ref[...]


def _wkv(sh, dr, s0):
    bsz, seq, c3 = sh.shape
    L = CHUNK
    assert L == HEAD_DIM
    cps = min(CHUNKS_PER_STEP, seq // L)
    tb = cps * L
    assert seq % tb == 0
    ns = seq // tb
    kern = functools.partial(_wkv_kernel, L=L, cps=cps, nsteps=ns)
    st_shape = (1, 2, N_GROUPS, HEAD_DIM, GW)
    return pl.pallas_call(
        kern,
        grid=(bsz, ns),
        in_specs=[pl.BlockSpec((1, tb, c3), lambda b, j: (b, j, 0)),
                  pl.BlockSpec((1, tb, c3), lambda b, j: (b, ns - 1 - j, 0)),
                  pl.BlockSpec((1, 1, tb, c3), lambda b, j: (0, b, j, 0)),
                  pl.BlockSpec((1, 1, tb, c3), lambda b, j: (1, b, ns - 1 - j, 0)),
                  pl.BlockSpec(st_shape, lambda b, j: (b, 0, 0, 0, 0))],
        out_specs=[pl.BlockSpec((1, tb, D_RWKV), lambda b, j: (b, j, 0)),
                   pl.BlockSpec((1, tb, D_RWKV), lambda b, j: (b, ns - 1 - j, 0)),
                   pl.BlockSpec(st_shape, lambda b, j: (b, 0, 0, 0, 0))],
        out_shape=[jax.ShapeDtypeStruct((bsz, seq, D_RWKV), F32),
                   jax.ShapeDtypeStruct((bsz, seq, D_RWKV), F32),
                   jax.ShapeDtypeStruct((bsz, 2, N_GROUPS, HEAD_DIM, GW), F32)],
        scratch_shapes=[pltpu.VMEM((2, N_GROUPS, HEAD_DIM, GW), F32)],
        compiler_params=_params(2),
        name="wkv",
    )(sh, sh, dr, dr, s0)


def _ffn_chain(x, mod_ref, gn_ref, wup_ref, cw_ref, wdn_ref, tm, row_len, store):
    h = _norm_mod(x, gn_ref[2:3, :], mod_ref[0, 3:4, :], mod_ref[0, 4:5, :]).astype(BF16)
    first, last = _row_edges(tm, row_len)
    yo = None
    for c0, c1 in FF_SLABS:
        a = jnp.dot(h, wup_ref[0, :, c0:c1], preferred_element_type=F32)
        gate = jnp.dot(h, wup_ref[0, :, D_FF + c0:D_FF + c1], preferred_element_type=F32)
        yield
        c = _dwconv3(a, cw_ref[0, :, c0:c1], first, last, tm)
        u = (c * _sigmoid(c)) * gate
        part = jnp.dot(u.astype(BF16), wdn_ref[0, c0:c1, :], preferred_element_type=F32)
        yield
        yo = part if yo is None else yo + part
    store(x + mod_ref[0, 5:6, :] * _rms(yo, gn_ref[3:4, :]))


def _ffn_specs(layer):
    return [_layer_spec((D_MODEL, 2 * D_FF), layer), _layer_spec((3, D_FF), layer),
            _layer_spec((D_FF, D_MODEL), layer)]


def _token_tiling(x, mod, row_len):
    bsz, seq, _ = x.shape
    tm = min(TM_MIX, seq)
    sub = min(SUB_MIX, tm)
    assert seq % tm == 0 and tm % sub == 0 and sub % row_len == 0 and row_len & (row_len - 1) == 0
    assert all(c % MXU_DIM == 0 for c, _ in FF_SLABS) and FF_SLABS[-1][1] == D_FF and D_RWKV % MXU_DIM == 0
    mod_map = (lambda b, j: (b, 0, 0)) if mod.shape[0] == bsz else (lambda b, j: (0, 0, 0))
    return bsz, seq // tm, tm, sub, pl.BlockSpec((1, 6, D_MODEL), mod_map)


def _even_out_chain(rows, x_ref, yf_ref, yb_ref, mg_ref, mod_ref, gn_ref, vec_ref, e_ref,
                    icnt_ref, pw_ref, wout_ref, wup_ref, cw_ref, wdn_ref, o_ref, tm, pt, row_len):
    y = yf_ref[0, rows, :] + yb_ref[0, rows, :]
    inv_n = 1.0 / HEAD_DIM
    mean = _head_sums(y, e_ref[...]) * inv_n
    yield
    yc = y - mean
    var = _head_sums(yc * yc, e_ref[...]) * inv_n
    yield
    yn = yc * lax.rsqrt(var + GN_EPS) * vec_ref[7:8, :] + vec_ref[8:9, :]
    y_rwkv = (yn + mg_ref[0, rows, 0:D_RWKV]) * mg_ref[0, rows, D_RWKV:2 * D_RWKV]

    pp = mg_ref[0, rows, 2 * D_RWKV:3 * D_RWKV]
    icnt = jnp.concatenate([icnt_ref[...]] * (tm // pt), axis=0)
    dpool = _window_sums(pp, tm, row_len) * icnt - pp
    outs = []
    for gi in range(len(POOL_WINDOWS)):
        sl = slice(gi * POOL_GROUP, (gi + 1) * POOL_GROUP)
        outs.append(_bdot(dpool[:, sl], pw_ref[gi]))
    yield
    y_pool = jnp.concatenate(outs, axis=-1) * vec_ref[9:10, :]

    cat = jnp.concatenate([y_rwkv, y_pool], axis=-1)
    yo = _bdot(cat, wout_ref[...])
    yield
    x1 = x_ref[0, rows, :] + mod_ref[0, 2:3, :] * _rms(yo, gn_ref[1:2, :])

    def store(val):
        o_ref[0, rows, :] = val

    yield from _ffn_chain(x1, mod_ref, gn_ref, wup_ref, cw_ref, wdn_ref, tm, row_len, store)


def _even_out_kernel(*refs, tm, sub, pt, row_len):
    _round_robin([_even_out_chain(slice(s * sub, (s + 1) * sub), *refs, sub, pt, row_len)
                  for s in range(tm // sub)])


def _even_out_ffn(x, yf, yb, mg, mod, gn, ew, pc, fw, row_len):
    bsz, nt, tm, sub, mod_spec = _token_tiling(x, mod, row_len)
    c3 = 3 * D_RWKV
    tok = lambda w: pl.BlockSpec((1, tm, w), lambda b, j: (b, j, 0))
    pt = pc["icnt"].shape[0]
    assert sub % pt == 0 and pt % row_len == 0
    assert all(POOL_WINDOWS[i] == 2 << i for i in range(len(POOL_WINDOWS)))
    return pl.pallas_call(
        functools.partial(_even_out_kernel, tm=tm, sub=sub, pt=pt, row_len=row_len),
        grid=(bsz, nt),
        in_specs=[tok(D_MODEL), tok(D_RWKV), tok(D_RWKV), tok(c3), mod_spec,
                  _const_spec((4, D_MODEL)),
                  _const_spec((16, D_RWKV)),
                  _const_spec((MXU_DIM, MXU_DIM)),
                  _const_spec((pt, D_POOL)),
                  _const_spec((4, POOL_GROUP, POOL_GROUP)),
                  _const_spec((D_MODEL, D_MODEL))] + _ffn_specs(fw["layer"]),
        out_specs=tok(D_MODEL),
        out_shape=jax.ShapeDtypeStruct(x.shape, F32),
        compiler_params=_params(2),
        name="even_out_ffn",
    )(x, yf, yb, mg, mod, gn, ew["vec"], ew["seg"], pc["icnt"], ew["pool_w"], ew["w_out"],
      fw["w_up"], fw["conv"], fw["w_down"])


def _odd_chain(rows, x_ref, mod_ref, gn_ref, win_ref, cw_ref, wout_ref, fup_ref, fcw_ref, fdn_ref, o_ref,
               tm, row_len):
    x = x_ref[0, rows, :]
    h = _norm_mod(x, gn_ref[0:1, :], mod_ref[0, 0:1, :], mod_ref[0, 1:2, :])
    p = jnp.dot(h.astype(BF16), win_ref[0], preferred_element_type=F32)
    yield
    bg = p[:, 0:D_MODEL]
    cg = p[:, D_MODEL:2 * D_MODEL]
    u = p[:, 2 * D_MODEL:3 * D_MODEL]
    first, last = _row_edges(tm, row_len)
    z = bg * _dwconv3(cg * u, cw_ref[0], first, last, tm)
    yo = _bdot(z, wout_ref[0])
    yield
    x1 = x + mod_ref[0, 2:3, :] * _rms(yo, gn_ref[1:2, :])

    def store(val):
        o_ref[0, rows, :] = val

    yield from _ffn_chain(x1, mod_ref, gn_ref, fup_ref, fcw_ref, fdn_ref, tm, row_len, store)


def _odd_kernel(*refs, tm, sub, row_len):
    _round_robin([_odd_chain(slice(s * sub, (s + 1) * sub), *refs, sub, row_len) for s in range(tm // sub)])


def _odd_mix_ffn(x, mod, gn, ow, fw, row_len):
    bsz, nt, tm, sub, mod_spec = _token_tiling(x, mod, row_len)
    tok = pl.BlockSpec((1, tm, D_MODEL), lambda b, j: (b, j, 0))
    return pl.pallas_call(
        functools.partial(_odd_kernel, tm=tm, sub=sub, row_len=row_len),
        grid=(bsz, nt),
        in_specs=[tok, mod_spec,
                  _const_spec((4, D_MODEL)),
                  _layer_spec((D_MODEL, 3 * D_MODEL), ow["layer"]),
                  _layer_spec((3, D_MODEL), ow["layer"]),
                  _layer_spec((D_MODEL, D_MODEL), ow["layer"])] + _ffn_specs(fw["layer"]),
        out_specs=tok,
        out_shape=jax.ShapeDtypeStruct(x.shape, F32),
        compiler_params=_params(2),
        name="odd_mix_ffn",
    )(x, mod, gn, ow["w_in"], ow["conv"], ow["w_out"], fw["w_up"], fw["conv"], fw["w_down"])


def _pool_consts(tm, row_len):
    col = np.arange(tm) % row_len
    icnt = np.zeros((tm, D_POOL), np.float32)
    for gi, win in enumerate(POOL_WINDOWS):
        lo = np.clip(col - win // 2, 0, row_len)
        hi = np.clip(col + win // 2, 0, row_len)
        icnt[:, gi * POOL_GROUP:(gi + 1) * POOL_GROUP] = (1.0 / (hi - lo).astype(np.float32))[:, None]
    return {"icnt": jnp.asarray(icnt, F32)}


def _even_weights(i, ev_w_in, ev_w_out, ev_mu, ev_w0, ev_w_up, ev_a0, ev_a_up, ev_g_up, ev_k_k,
                  ev_k_a, ev_r_k, ev_gn_w, ev_gn_b, ev_pool_w, ev_pool_scale):
    o_lw = 3 * D_RWKV
    o_la = o_lw + 2 * D_DECAY_LORA
    o_lg = o_la + 2 * D_AAA_LORA
    o_pool = o_lg + D_GATE_LORA

    def repack(a, with_pool):
        z = lambda n: jnp.zeros(a.shape[:-1] + (n,), a.dtype)
        parts = [a[..., :o_lw], a[..., o_lw:o_la], z(LANES - 2 * D_DECAY_LORA),
                 a[..., o_la:o_lg], a[..., o_lg:o_pool], z(LANES - D_GATE_LORA)]
        if with_pool:
            parts.append(a[..., o_pool:])
        return jnp.concatenate(parts, axis=-1)

    def per_dir(w):
        rows = jnp.concatenate([jnp.pad(w[0], ((0, 0), (0, D_RWKV))), jnp.pad(w[1], ((0, 0), (D_RWKV, 0)))], axis=0)
        return jnp.pad(rows, ((0, LANES - rows.shape[0]), (0, 0)))

    wup = per_dir(ev_w_up[i])
    aup = per_dir(ev_a_up[i])
    gup = jnp.pad(ev_g_up[i], ((0, LANES - D_GATE_LORA), (0, 0)))
    row = lambda a: a.reshape(1, D_RWKV)
    vec = jnp.concatenate([ev_w0[i], ev_a0[i], row(ev_k_k[i]), row(ev_k_a[i]), row(ev_r_k[i]), row(ev_gn_w[i]),
                           row(ev_gn_b[i]), row(ev_pool_scale[i]), jnp.zeros((6, D_RWKV), F32)], axis=0)
    head = np.arange(MXU_DIM) // HEAD_DIM
    seg = jnp.asarray(head[:, None] == head[None, :], BF16)
    return {"w_in": repack(ev_w_in[i], True).astype(BF16), "mu": repack(ev_mu[i], False),
            "wup": wup.astype(BF16), "aup": aup.astype(BF16), "gup": gup.astype(BF16),
            "vec": vec, "seg": seg, "pool_w": ev_pool_w[i].astype(BF16),
            "w_out": ev_w_out[i].astype(BF16)}


def kernel(x, c, ctx, c_ctx, w_mod, b_mod, norm_g, ffn_w_up, ffn_conv, ffn_w_down, ev_w_in, ev_w_out,
           ev_mu, ev_w0, ev_w_up, ev_a0, ev_a_up, ev_g_up, ev_k_k, ev_k_a, ev_r_k, ev_gn_w, ev_gn_b,
           ev_pool_w, ev_pool_scale, od_w_in, od_conv, od_w_out):
    bsz, seq, _ = x.shape
    ctx_len = ctx.shape[1]
    assert SUB_MIX % POOL_TILE == 0 and POOL_TILE % GRID_W == 0
    assert ctx_len <= min(SUB_MIX, TM_EVEN_IN) and ctx_len % CHUNK == 0 and seq % (CHUNK * CHUNKS_PER_STEP) == 0 and bsz < 16

    cv = jnp.concatenate([c, c_ctx[None], jnp.zeros((15 - bsz, D_MODEL), F32)], axis=0)
    mods = _mods(cv, w_mod, b_mod)
    pc_lat = _pool_consts(POOL_TILE, GRID_W)
    pc_ctx = _pool_consts(ctx_len, ctx_len)
    ffn_w_up_b = ffn_w_up.astype(BF16)
    ffn_w_down_b = ffn_w_down.astype(BF16)
    od_w_in_b = od_w_in.astype(BF16)
    od_w_out_b = od_w_out.astype(BF16)

    for layer in range(DEPTH):
        i = layer // 2
        even = layer % 2 == 0
        ctx_later = any(jj % 2 == 0 for jj in range(layer + 1, DEPTH))
        mod = mods[layer, :bsz].reshape(bsz, 6, D_MODEL)
        mod_c = mods[layer, bsz:bsz + 1].reshape(1, 6, D_MODEL)
        gn = norm_g[layer]
        fw = {"w_up": ffn_w_up_b, "conv": ffn_conv, "w_down": ffn_w_down_b, "layer": layer}
        if even:
            ew = _even_weights(i, ev_w_in, ev_w_out, ev_mu, ev_w0, ev_w_up, ev_a0, ev_a_up, ev_g_up,
                               ev_k_k, ev_k_a, ev_r_k, ev_gn_w, ev_gn_b, ev_pool_w, ev_pool_scale)
            sh_c, dr_c, mg_c = _even_in(ctx, mod_c, gn, ew)
            sh_l, dr_l, mg_l = _even_in(x, mod, gn, ew)
            zero = jnp.zeros((bsz, 2, N_GROUPS, HEAD_DIM, GW), F32)
            yf_c, yb_c, s_c = _wkv(sh_c, dr_c, zero)
            yf_l, yb_l, _ = _wkv(sh_l, dr_l, s_c)
            x = _even_out_ffn(x, yf_l, yb_l, mg_l, mod, gn, ew, pc_lat, fw, GRID_W)
            if ctx_later:
                ctx = _even_out_ffn(ctx, yf_c, yb_c, mg_c, mod_c, gn, ew, pc_ctx, fw, ctx_len)
        else:
            ow = {"w_in": od_w_in_b, "conv": od_conv, "w_out": od_w_out_b, "layer": i}
            x = _odd_mix_ffn(x, mod, gn, ow, fw, GRID_W)
            if ctx_later:
                ctx = _odd_mix_ffn(ctx, mod_c, gn, ow, fw, ctx_len)
    return x
```
